```python
import math
import jax, jax.numpy as jnp
from jax import lax
import numpy as np


D_MODEL = 4096
BATCH = 4
SEQ = 2048
DEPTH = 4
DEC_BATCH = 8
DEC_SEQ = 1
PAST_LEN = 8192
PAGE_SIZE = 128

HEAD_DIM = 128
CONV_CH = D_MODEL // 4
CONV_WIDTH = 31
CHUNK_CH = D_MODEL // 4
CHUNK_GROUPS = CHUNK_CH // HEAD_DIM
CHUNK = 128
ATT_W = D_MODEL // 2
ATT_HEADS = ATT_W // (2 * HEAD_DIM)
ROT_DIM = HEAD_DIM // 4
ROPE_THETA = 500000.0
Q_BLOCK = 128
MIX_W = CONV_CH + CHUNK_CH + ATT_W
D_FF = -(-8 * D_MODEL // (3 * 256)) * 256
SPLITS = (2 * CONV_CH, 2 * CONV_CH + 2 * CHUNK_CH, 2 * CONV_CH + 2 * CHUNK_CH + ATT_W,
          2 * CONV_CH + 2 * CHUNK_CH + 2 * ATT_W)
IN_COLS = 2 * CONV_CH + 2 * CHUNK_CH + 3 * ATT_W

kernel_name = "hymba_conv_gmlp_diffattn_step"


def rmsnorm(x, g, eps=1e-6):
    xf = x.astype(jnp.float32)
    y = xf * lax.rsqrt(jnp.mean(xf * xf, axis=-1, keepdims=True) + eps)
    return (y * g.astype(jnp.float32)).astype(x.dtype)


def layernorm(x, g, b, eps=1e-5):
    xf = x.astype(jnp.float32)
    mu = jnp.mean(xf, axis=-1, keepdims=True)
    xc = xf - mu
    var = jnp.mean(xc * xc, axis=-1, keepdims=True)
    return (xc * lax.rsqrt(var + eps) * g.astype(jnp.float32) + b.astype(jnp.float32)).astype(x.dtype)


def rope_partial(x, pos):
    half = ROT_DIM // 2
    inv = jnp.power(ROPE_THETA, -(jnp.arange(half, dtype=jnp.float32) * (2.0 / ROT_DIM)))
    ang = pos.astype(jnp.float32)[:, None] * inv[None, :]
    cos = jnp.cos(ang)[None, :, None, None, :]
    sin = jnp.sin(ang)[None, :, None, None, :]
    xr = x[..., :ROT_DIM].astype(jnp.float32)
    x1, x2 = xr[..., :half], xr[..., half:]
    rot = jnp.concatenate([x1 * cos - x2 * sin, x2 * cos + x1 * sin], axis=-1).astype(x.dtype)
    return jnp.concatenate([rot, x[..., ROT_DIM:]], axis=-1)


def conv_mixer(a, prefix, w_dw, b_dw, ln_g, ln_b, w_pw2):
    g = a[..., :CONV_CH] * jax.nn.sigmoid(a[..., CONV_CH:])
    full = jnp.concatenate([prefix.astype(g.dtype), g], axis=1)
    y = lax.conv_general_dilated(full, w_dw[:, None, :].astype(full.dtype), window_strides=(1,),
                                 padding='VALID', dimension_numbers=('NWC', 'WIO', 'NWC'),
                                 feature_group_count=CONV_CH) + b_dw
    y = jax.nn.silu(layernorm(y, ln_g, ln_b))
    return y @ w_pw2, full[:, -(CONV_WIDTH - 1):]


def chunk_mixer(z, ln_g, ln_b, w_s, b_s):
    z = jax.nn.gelu(z)
    u = z[..., :CHUNK_CH]
    v = layernorm(z[..., CHUNK_CH:], ln_g, ln_b)
    B, T, _ = v.shape
    n_chunks = -(-T // CHUNK)
    vp = jnp.pad(v, ((0, 0), (0, n_chunks * CHUNK - T), (0, 0)))
    vp = vp.reshape(B, n_chunks, CHUNK, CHUNK_GROUPS, HEAD_DIM)
    causal = jnp.tril(jnp.ones((CHUNK, CHUNK), dtype=bool))
    ws = jnp.where(causal[None], w_s, 0).astype(vp.dtype)
    mixed = jnp.einsum('gts,bnsgc->bntgc', ws, vp) + b_s.T[None, None, :, :, None]
    mixed = mixed.reshape(B, n_chunks * CHUNK, CHUNK_CH)[:, :T]
    return u * mixed, v


def diff_attention_core(q, k, v, q_pos, k_pos, lam):
    s = jnp.einsum('bqhcd,bkhcd->bhcqk', q, k, preferred_element_type=jnp.float32) * (HEAD_DIM ** -0.5)
    s = jnp.where(k_pos[None, :] <= q_pos[:, None], s, -jnp.inf)
    p = jax.nn.softmax(s, axis=-1)
    a = p[:, :, 0] - lam * p[:, :, 1]
    return jnp.einsum('bhqk,bkhe->bqhe', a.astype(v.dtype), v)


def diff_head_norm(o, g, lam_init):
    of = o.astype(jnp.float32)
    of = of * lax.rsqrt(jnp.mean(of * of, axis=-1, keepdims=True) + 1e-5) * g.astype(jnp.float32)
    return (of * (1.0 - lam_init)).astype(o.dtype)


def _layer(x, p, lam_init, conv_prefix, past_k, past_v, pos, k_pos, prompt):
    B, T, _ = x.shape
    h = rmsnorm(x, p['norm1'])
    z = h @ p['w_in']
    za, zb, zq, zk, zv = jnp.split(z, SPLITS, axis=-1)
    ya, conv_new = conv_mixer(za, conv_prefix, p['w_dw'], p['b_dw'], p['conv_ln_g'], p['conv_ln_b'], p['w_pw2'])
    yb, v_rows = chunk_mixer(zb, p['sgu_ln_g'], p['sgu_ln_b'], p['w_s'], p['b_s'])
    q = rope_partial(zq.reshape(B, T, ATT_HEADS, 2, HEAD_DIM), pos)
    k = rope_partial(zk.reshape(B, T, ATT_HEADS, 2, HEAD_DIM), pos)
    v = zv.reshape(B, T, ATT_HEADS, 2 * HEAD_DIM)
    f32 = jnp.float32
    lam = (jnp.exp(jnp.sum(p['lq1'].astype(f32) * p['lk1'].astype(f32)))
           - jnp.exp(jnp.sum(p['lq2'].astype(f32) * p['lk2'].astype(f32))) + lam_init)
    if prompt:
        nb = T // Q_BLOCK
        qb = jnp.moveaxis(q.reshape(B, nb, Q_BLOCK, ATT_HEADS, 2, HEAD_DIM), 1, 0)
        pb = pos.reshape(nb, Q_BLOCK)
        o = lax.map(lambda qp: diff_attention_core(qp[0], k, v, qp[1], k_pos, lam), (qb, pb))
        o = jnp.moveaxis(o, 0, 1).reshape(B, T, ATT_HEADS, 2 * HEAD_DIM)
    else:
        keys = jnp.concatenate([past_k.astype(k.dtype), k], axis=1)
        vals = jnp.concatenate([past_v.astype(v.dtype), v], axis=1)
        o = diff_attention_core(q, keys, vals, pos, k_pos, lam)
    yc = diff_head_norm(o, p['subln_g'], lam_init).reshape(B, T, ATT_W)
    x = x + jnp.concatenate([ya, yb.astype(ya.dtype), yc.astype(ya.dtype)], axis=-1) @ p['w_o']
    h2 = rmsnorm(x, p['norm2'])
    x = x + (jax.nn.silu(h2 @ p['w_gate']) * (h2 @ p['w_up'])) @ p['w_down']
    return x, conv_new, k, v, v_rows


def setup_inputs(seed: int = 0) -> dict:
    key = jax.random.key(seed)
    ks = jax.random.split(key, 32)
    f32 = jnp.float32
    n_pages = PAST_LEN // PAGE_SIZE
    n_used = DEC_BATCH * n_pages
    n_pool = n_used + max(1, n_used // 4)
    nrm = lambda k, shape, s: jax.random.normal(k, shape, f32) * s
    page_table = jax.random.permutation(ks[5], n_pool)[:n_used].reshape(DEC_BATCH, n_pages).astype(jnp.int32)
    return {
        'x_prompt': nrm(ks[0], (BATCH, SEQ, D_MODEL), 1.0),
        'x_sample': nrm(ks[1], (DEC_BATCH, DEC_SEQ, D_MODEL), 1.0),
        'cache_k': nrm(ks[2], (DEPTH, n_pool, PAGE_SIZE, ATT_HEADS, 2, HEAD_DIM), 1.0),
        'cache_v': nrm(ks[3], (DEPTH, n_pool, PAGE_SIZE, ATT_HEADS, 2 * HEAD_DIM), 1.0),
        'state_conv': nrm(ks[4], (DEPTH, DEC_BATCH, CONV_WIDTH - 1, CONV_CH), 0.5),
        'page_table': page_table,
        'norm1_g': 1.0 + nrm(ks[6], (DEPTH, D_MODEL), 0.01),
        'w_in': nrm(ks[7], (DEPTH, D_MODEL, IN_COLS), D_MODEL ** -0.5),
        'w_dw': nrm(ks[8], (DEPTH, CONV_WIDTH, CONV_CH), CONV_WIDTH ** -0.5),
        'b_dw': nrm(ks[9], (DEPTH, CONV_CH), 0.01),
        'conv_ln_g': 1.0 + nrm(ks[10], (DEPTH, CONV_CH), 0.01),
        'conv_ln_b': nrm(ks[11], (DEPTH, CONV_CH), 0.01),
        'w_pw2': nrm(ks[12], (DEPTH, CONV_CH, CONV_CH), CONV_CH ** -0.5),
        'sgu_ln_g': 1.0 + nrm(ks[13], (DEPTH, CHUNK_CH), 0.01),
        'sgu_ln_b': nrm(ks[14], (DEPTH, CHUNK_CH), 0.01),
        'w_s': nrm(ks[15], (DEPTH, CHUNK_GROUPS, CHUNK, CHUNK), CHUNK ** -0.5),
        'b_s': 1.0 + nrm(ks[16], (DEPTH, CHUNK_GROUPS, CHUNK), 0.1),
        'lambda_q1': nrm(ks[17], (DEPTH, HEAD_DIM), 0.1),
        'lambda_k1': nrm(ks[18], (DEPTH, HEAD_DIM), 0.1),
        'lambda_q2': nrm(ks[19], (DEPTH, HEAD_DIM), 0.1),
        'lambda_k2': nrm(ks[20], (DEPTH, HEAD_DIM), 0.1),
        'subln_g': 1.0 + nrm(ks[21], (DEPTH, 2 * HEAD_DIM), 0.01),
        'w_o': nrm(ks[22], (DEPTH, MIX_W, D_MODEL), MIX_W ** -0.5),
        'norm2_g': 1.0 + nrm(ks[23], (DEPTH, D_MODEL), 0.01),
        'w_gate': nrm(ks[24], (DEPTH, D_MODEL, D_FF), D_MODEL ** -0.5),
        'w_up': nrm(ks[25], (DEPTH, D_MODEL, D_FF), D_MODEL ** -0.5),
        'w_down': nrm(ks[26], (DEPTH, D_FF, D_MODEL), D_FF ** -0.5),
        'final_g': 1.0 + nrm(ks[27], (D_MODEL,), 0.01),
    }


def reference(x_prompt, x_sample, cache_k, cache_v, state_conv, page_table, norm1_g, w_in, w_dw, b_dw,
              conv_ln_g, conv_ln_b, w_pw2, sgu_ln_g, sgu_ln_b, w_s, b_s, lambda_q1, lambda_k1, lambda_q2,
              lambda_k2, subln_g, w_o, norm2_g, w_gate, w_up, w_down, final_g):
    B, T, _ = x_prompt.shape
    DB, TS, _ = x_sample.shape
    n_pages = page_table.shape[1]
    past = n_pages * PAGE_SIZE
    pos_p = jnp.arange(T, dtype=jnp.int32)
    pos_s = past + jnp.arange(TS, dtype=jnp.int32)
    kpos_s = jnp.arange(past + TS, dtype=jnp.int32)
    hp, hs = x_prompt, x_sample
    kp_l, vp_l, cp_l, ks_l, vs_l, cs_l, chs_l = [], [], [], [], [], [], []
    for l in range(DEPTH):
        lam_init = 0.8 - 0.6 * math.exp(-0.3 * l)
        p = {'norm1': norm1_g[l], 'w_in': w_in[l], 'w_dw': w_dw[l], 'b_dw': b_dw[l],
             'conv_ln_g': conv_ln_g[l], 'conv_ln_b': conv_ln_b[l], 'w_pw2': w_pw2[l],
             'sgu_ln_g': sgu_ln_g[l], 'sgu_ln_b': sgu_ln_b[l], 'w_s': w_s[l], 'b_s': b_s[l],
             'lq1': lambda_q1[l], 'lk1': lambda_k1[l], 'lq2': lambda_q2[l], 'lk2': lambda_k2[l],
             'subln_g': subln_g[l], 'w_o': w_o[l], 'norm2': norm2_g[l],
             'w_gate': w_gate[l], 'w_up': w_up[l], 'w_down': w_down[l]}
        zero_prefix = jnp.zeros((B, CONV_WIDTH - 1, CONV_CH), x_prompt.dtype)
        hp, conv_p, k_p, v_p, _ = _layer(hp, p, lam_init, zero_prefix, None, None, pos_p, pos_p, True)
        past_k = cache_k[l][page_table].reshape(DB, past, ATT_HEADS, 2, HEAD_DIM)
        past_v = cache_v[l][page_table].reshape(DB, past, ATT_HEADS, 2 * HEAD_DIM)
        hs, conv_s, k_s, v_s, chv_s = _layer(hs, p, lam_init, state_conv[l], past_k, past_v, pos_s, kpos_s, False)
        kp_l.append(k_p); vp_l.append(v_p); cp_l.append(conv_p)
        ks_l.append(k_s); vs_l.append(v_s); cs_l.append(conv_s); chs_l.append(chv_s)
    y_prompt = rmsnorm(hp, final_g)
    y_sample = rmsnorm(hs, final_g)
    k_prompt = jnp.stack(kp_l)
    v_prompt = jnp.stack(vp_l)
    conv_prompt = jnp.stack(cp_l)
    k_sample = jnp.stack(ks_l)
    v_sample = jnp.stack(vs_l)
    conv_sample = jnp.stack(cs_l)
    chunk_v_sample = jnp.stack(chs_l)
    return (y_prompt, y_sample, k_prompt, v_prompt, conv_prompt, k_sample, v_sample, conv_sample, chunk_v_sample)
```

```python
import functools
import math

import jax
import jax.numpy as jnp
from jax import lax
from jax.experimental import pallas as pl
from jax.experimental.pallas import tpu as pltpu

F32 = jnp.float32
BF16 = jnp.bfloat16

HEAD_DIM = 128
CONV_WIDTH = 31
CHUNK = 128
PAGE_SIZE = 128
ROT_DIM = HEAD_DIM // 4
ROPE_THETA = 500000.0
LANES = 128
CONV_HALO = 32
VMEM_LIMIT = 60 * 1024 * 1024


def _cparams(sem):
    return pltpu.CompilerParams(dimension_semantics=sem, vmem_limit_bytes=VMEM_LIMIT)


def _sigmoid(x):
    return 1.0 / (1.0 + jnp.exp(-x))


def _gelu_tanh(x):
    c = math.sqrt(2.0 / math.pi)
    return x * (0.5 * (1.0 + jnp.tanh(c * (x + 0.044715 * (x * x * x)))))


def _layernorm(x, g, b, eps=1e-5):
    mu = jnp.mean(x, axis=-1, keepdims=True)
    xc = x - mu
    var = jnp.mean(xc * xc, axis=-1, keepdims=True)
    return xc * lax.rsqrt(var + eps) * g + b


def _rmsnorm_body(x_ref, g_ref, o_ref, *, eps):
    x = x_ref[...]
    ms = jnp.mean(x * x, axis=-1, keepdims=True)
    o_ref[...] = (x * lax.rsqrt(ms + eps) * g_ref[...]).astype(o_ref.dtype)


def _rmsnorm(x, g, out_dtype, eps=1e-6):
    m, d = x.shape
    tm = min(256, m)
    return pl.pallas_call(
        functools.partial(_rmsnorm_body, eps=eps),
        grid=(m // tm,),
        in_specs=[pl.BlockSpec((tm, d), lambda i: (i, 0)),
                  pl.BlockSpec((1, d), lambda i: (0, 0))],
        out_specs=pl.BlockSpec((tm, d), lambda i: (i, 0)),
        out_shape=jax.ShapeDtypeStruct((m, d), out_dtype),
        compiler_params=_cparams(("parallel",)),
        name="rmsnorm",
    )(x, g.reshape(1, d))


def _rope_group(x, a, bm, bp):
    return x * a + pltpu.roll(x, LANES - ROT_DIM // 2, 1) * bm + pltpu.roll(x, ROT_DIM // 2, 1) * bp


def _mm_body(*refs, n_lhs, has_res, has_rope, n_out, tn):
    xs = refs[:n_lhs]
    ws = refs[n_lhs:2 * n_lhs]
    pos = 2 * n_lhs
    res_ref = None
    if has_res:
        res_ref = refs[pos]
        pos += 1
    if has_rope:
        ra, rm, rp = refs[pos:pos + 3]
        pos += 3
    outs = refs[pos:pos + n_out]
    acc = None
    for x_ref, w_ref in zip(xs, ws):
        d = jnp.dot(x_ref[...].astype(BF16), w_ref[...], preferred_element_type=F32)
        acc = d if acc is None else acc + d
    if has_res:
        acc = res_ref[...] + acc
    if has_rope:
        a, bm, bp = ra[...], rm[...], rp[...]
        for g in range(tn // LANES):
            sl = slice(g * LANES, (g + 1) * LANES)
            piece = _rope_group(acc[:, sl], a, bm, bp)
            for o in outs:
                o[:, sl] = piece.astype(o.dtype)
    else:
        for o in outs:
            o[...] = acc.astype(o.dtype)


def _matmul(xs, w, layer, *, k_sizes, n_off, n, out_dtypes, res=None, rope=None, rope_rows=None, name):
    m = xs[0].shape[0]
    tm = min(1024, m)
    tn = min(1024, n)
    assert m % tm == 0 and n % tn == 0 and n_off % tn == 0
    in_specs, args = [], []
    for x, ks in zip(xs, k_sizes):
        in_specs.append(pl.BlockSpec((tm, ks), lambda i, j: (i, 0)))
        args.append(x)
    off = 0
    for ks in k_sizes:
        assert off % ks == 0
        in_specs.append(pl.BlockSpec((None, ks, tn), lambda i, j, kb=off // ks: (layer, kb, j + n_off // tn)))
        args.append(w)
        off += ks
    if res is not None:
        in_specs.append(pl.BlockSpec((tm, tn), lambda i, j: (i, j)))
        args.append(res)
    if rope is not None:
        nrb = rope_rows // tm if rope_rows >= tm else 1
        assert rope_rows % tm == 0 or rope_rows == tm
        for t in rope:
            in_specs.append(pl.BlockSpec((tm, LANES), lambda i, j: (i % nrb, 0)))
            args.append(t)
    body = functools.partial(_mm_body, n_lhs=len(xs), has_res=res is not None, has_rope=rope is not None,
                             n_out=len(out_dtypes), tn=tn)
    outs = pl.pallas_call(
        body,
        grid=(m // tm, n // tn),
        in_specs=in_specs,
        out_specs=[pl.BlockSpec((tm, tn), lambda i, j: (i, j)) for _ in out_dtypes],
        out_shape=[jax.ShapeDtypeStruct((m, n), dt) for dt in out_dtypes],
        compiler_params=_cparams(("parallel", "parallel")),
        name=name,
    )(*args)
    return outs


def _gateup_body(x_ref, wg_ref, wu_ref, o_ref):
    x = x_ref[...].astype(BF16)
    g = jnp.dot(x, wg_ref[...], preferred_element_type=F32)
    u = jnp.dot(x, wu_ref[...], preferred_element_type=F32)
    o_ref[...] = (g * _sigmoid(g) * u).astype(o_ref.dtype)


def _gateup(x, wg, wu, layer):
    m, d = x.shape
    fp = wg.shape[2]
    tm = min(1024, m)
    tf = 512
    assert fp % tf == 0 and m % tm == 0
    return pl.pallas_call(
        _gateup_body,
        grid=(m // tm, fp // tf),
        in_specs=[pl.BlockSpec((tm, d), lambda i, j: (i, 0)),
                  pl.BlockSpec((None, d, tf), lambda i, j: (layer, 0, j)),
                  pl.BlockSpec((None, d, tf), lambda i, j: (layer, 0, j))],
        out_specs=pl.BlockSpec((tm, tf), lambda i, j: (i, j)),
        out_shape=jax.ShapeDtypeStruct((m, fp), BF16),
        compiler_params=_cparams(("parallel", "parallel")),
        name="ffn_gateup",
    )(x, wg, wu)


def _down_body(a_ref, w_ref, res_ref, o_ref):
    @pl.when(pl.program_id(2) == 0)
    def _():
        o_ref[...] = res_ref[...]

    o_ref[...] += jnp.dot(a_ref[...], w_ref[...], preferred_element_type=F32)


def _down(a, wd, res, layer):
    m, fp = a.shape
    d = wd.shape[2]
    tm = min(1024, m)
    tn = min(2048, d)
    tk = 1024
    assert fp % tk == 0 and m % tm == 0 and d % tn == 0
    return pl.pallas_call(
        _down_body,
        grid=(m // tm, d // tn, fp // tk),
        in_specs=[pl.BlockSpec((tm, tk), lambda i, j, k: (i, k)),
                  pl.BlockSpec((None, tk, tn), lambda i, j, k: (layer, k, j)),
                  pl.BlockSpec((tm, tn), lambda i, j, k: (i, j))],
        out_specs=pl.BlockSpec((tm, tn), lambda i, j, k: (i, j)),
        out_shape=jax.ShapeDtypeStruct((m, d), F32),
        compiler_params=_cparams(("parallel", "parallel", "arbitrary")),
        name="ffn_down",
    )(a, wd, res)


def _conv_body(a_ref, gt_ref, ah_ref, gth_ref, wdw_ref, bdw_ref, lg_ref, lb_ref, wpw_ref,
               y_ref, tail_ref, buf, ybuf, *, tt, c, rows):
    t = pl.program_id(1)
    g = a_ref[0].astype(F32) * _sigmoid(gt_ref[0].astype(F32))
    gh = ah_ref[0].astype(F32) * _sigmoid(gth_ref[0].astype(F32))
    buf[0:CONV_HALO, :] = jnp.where(t > 0, gh, 0.0)
    buf[CONV_HALO:CONV_HALO + tt, :] = g
    base = CONV_HALO - (CONV_WIDTH - 1)
    for r0 in range(0, tt, rows):
        for c0 in range(0, c, LANES):
            acc = jnp.broadcast_to(bdw_ref[:, c0:c0 + LANES], (rows, LANES))
            for j in range(CONV_WIDTH):
                acc = acc + buf[r0 + base + j:r0 + base + j + rows, c0:c0 + LANES] * wdw_ref[j:j + 1, c0:c0 + LANES]
            ybuf[r0:r0 + rows, c0:c0 + LANES] = acc
    y = _layernorm(ybuf[...], lg_ref[...], lb_ref[...])
    y = y * _sigmoid(y)
    y_ref[0] = jnp.dot(y.astype(BF16), wpw_ref[...], preferred_element_type=F32).astype(y_ref.dtype)

    @pl.when(t == pl.num_programs(1) - 1)
    def _():
        tail_ref[0] = buf[CONV_HALO + tt - (CONV_WIDTH - 1):CONV_HALO + tt, :]


def _conv_mixer(zab, w_dw, b_dw, ln_g, ln_b, w_pw2, layer, *, bsz, seq, c):
    tt = min(256, seq)
    rows = min(64, tt)
    hb = tt // CONV_HALO
    body = functools.partial(_conv_body, tt=tt, c=c, rows=rows)
    vec = lambda: pl.BlockSpec((None, 1, c), lambda b, t: (layer, 0, 0))
    return pl.pallas_call(
        body,
        grid=(bsz, seq // tt),
        in_specs=[pl.BlockSpec((1, tt, c), lambda b, t: (b, t, 0)),
                  pl.BlockSpec((1, tt, c), lambda b, t: (b, t, 1)),
                  pl.BlockSpec((1, CONV_HALO, c), lambda b, t: (b, jnp.maximum(t * hb - 1, 0), 0)),
                  pl.BlockSpec((1, CONV_HALO, c), lambda b, t: (b, jnp.maximum(t * hb - 1, 0), 1)),
                  pl.BlockSpec((None, CONV_WIDTH, c), lambda b, t: (layer, 0, 0)),
                  vec(), vec(), vec(),
                  pl.BlockSpec((None, c, c), lambda b, t: (layer, 0, 0))],
        out_specs=[pl.BlockSpec((1, tt, c), lambda b, t: (b, t, 0)),
                   pl.BlockSpec((1, CONV_WIDTH - 1, c), lambda b, t: (b, 0, 0))],
        out_shape=[jax.ShapeDtypeStruct((bsz, seq, c), BF16),
                   jax.ShapeDtypeStruct((bsz, CONV_WIDTH - 1, c), F32)],
        scratch_shapes=[pltpu.VMEM((tt + CONV_HALO, c), F32), pltpu.VMEM((tt, c), F32)],
        compiler_params=_cparams(("parallel", "arbitrary")),
        name="conv_mixer",
    )(zab, zab, zab, zab, w_dw, b_dw, ln_g, ln_b, w_pw2)


def _chunk_body(u_ref, v_ref, lg_ref, lb_ref, ws_ref, bs_ref, o_ref, *, tt, groups):
    zu = _gelu_tanh(u_ref[...].astype(F32))
    v = _layernorm(_gelu_tanh(v_ref[...].astype(F32)), lg_ref[...], lb_ref[...]).astype(BF16)
    row = lax.broadcasted_iota(jnp.int32, (CHUNK, CHUNK), 0)
    col = lax.broadcasted_iota(jnp.int32, (CHUNK, CHUNK), 1)
    causal = col <= row
    bs = bs_ref[...]
    for g in range(groups):
        sl = slice(g * LANES, (g + 1) * LANES)
        wg = jnp.where(causal, ws_ref[g], 0.0).astype(BF16)
        for r0 in range(0, tt, CHUNK):
            mixed = jnp.dot(wg, v[r0:r0 + CHUNK, sl], preferred_element_type=F32) + bs[:, sl]
            o_ref[r0:r0 + CHUNK, sl] = (zu[r0:r0 + CHUNK, sl] * mixed).astype(o_ref.dtype)


def _chunk_mixer(zab, ln_g, ln_b, w_s, bs_full, layer, *, c):
    m = zab.shape[0]
    tt = min(256, m)
    groups = c // HEAD_DIM
    vec = lambda: pl.BlockSpec((None, 1, c), lambda i: (layer, 0, 0))
    return pl.pallas_call(
        functools.partial(_chunk_body, tt=tt, groups=groups),
        grid=(m // tt,),
        in_specs=[pl.BlockSpec((tt, c), lambda i: (i, 2)),
                  pl.BlockSpec((tt, c), lambda i: (i, 3)),
                  vec(), vec(),
                  pl.BlockSpec((None, groups, CHUNK, CHUNK), lambda i: (layer, 0, 0, 0)),
                  pl.BlockSpec((None, CHUNK, c), lambda i: (layer, 0, 0))],
        out_specs=pl.BlockSpec((tt, c), lambda i: (i, 0)),
        out_shape=jax.ShapeDtypeStruct((m, c), BF16),
        compiler_params=_cparams(("parallel",)),
        name="chunk_mixer",
    )(zab, zab, ln_g, ln_b, w_s, bs_full)


def _lambda_full(lq1, lk1, lq2, lk2, lam_init):
    s1 = jnp.sum(lq1[...] * lk1[...], axis=-1, keepdims=True)
    s2 = jnp.sum(lq2[...] * lk2[...], axis=-1, keepdims=True)
    return jnp.exp(s1) - jnp.exp(s2) + lam_init


def _head_norm(o, g, lam_init):
    return o * lax.rsqrt(jnp.mean(o * o, axis=-1, keepdims=True) + 1e-5) * g * (1.0 - lam_init)


def _attn_body(q_ref, k_ref, v_ref, lq1, lk1, lq2, lk2, g_ref, o_ref, m_sc, l_sc, acc_sc, *, tq, scale, lam_init):
    qi = pl.program_id(2)
    q = q_ref[0]

    def scores(c, kb):
        sl = slice(c * HEAD_DIM, (c + 1) * HEAD_DIM)
        return lax.dot_general(q[:, sl], kb[:, sl], (((1,), (1,)), ((), ())), preferred_element_type=F32) * scale

    start = pl.multiple_of(qi * tq, tq)
    kd = k_ref[0, pl.ds(start, tq), :]
    vd = v_ref[0, pl.ds(start, tq), :]
    row = lax.broadcasted_iota(jnp.int32, (tq, tq), 0)
    col = lax.broadcasted_iota(jnp.int32, (tq, tq), 1)
    causal = col <= row
    for c in range(2):
        s = jnp.where(causal, scores(c, kd), -jnp.inf)
        m = jnp.max(s, axis=1, keepdims=True)
        p = jnp.exp(s - m)
        m_sc[c] = m
        l_sc[c] = jnp.sum(p, axis=1, keepdims=True)
        acc_sc[c] = jnp.dot(p.astype(BF16), vd, preferred_element_type=F32)

    def body(kj, carry):
        off = pl.multiple_of(kj * tq, tq)
        kb = k_ref[0, pl.ds(off, tq), :]
        vb = v_ref[0, pl.ds(off, tq), :]
        for c in range(2):
            s = scores(c, kb)
            m_old = m_sc[c]
            m_new = jnp.maximum(m_old, jnp.max(s, axis=1, keepdims=True))
            alpha = jnp.exp(m_old - m_new)
            p = jnp.exp(s - m_new)
            l_sc[c] = alpha * l_sc[c] + jnp.sum(p, axis=1, keepdims=True)
            acc_sc[c] = alpha * acc_sc[c] + jnp.dot(p.astype(BF16), vb, preferred_element_type=F32)
            m_sc[c] = m_new
        return carry

    lax.fori_loop(0, qi, body, 0)
    lam = _lambda_full(lq1, lk1, lq2, lk2, lam_init)
    o = acc_sc[0] / l_sc[0] - lam * (acc_sc[1] / l_sc[1])
    o_ref[0] = _head_norm(o, g_ref[...], lam_init).astype(o_ref.dtype)


def _attention(q, k, v, lq1, lk1, lq2, lk2, subln_g, layer, lam_init, *, heads):
    bsz, seq, _ = q.shape
    tq = min(512, seq)
    hw = 2 * HEAD_DIM
    vec = lambda n: pl.BlockSpec((None, 1, n), lambda b, h, i: (layer, 0, 0))
    body = functools.partial(_attn_body, tq=tq, scale=HEAD_DIM ** -0.5, lam_init=lam_init)
    return pl.pallas_call(
        body,
        grid=(bsz, heads, seq // tq),
        in_specs=[pl.BlockSpec((1, tq, hw), lambda b, h, i: (b, i, h)),
                  pl.BlockSpec((1, seq, hw), lambda b, h, i: (b, 0, h)),
                  pl.BlockSpec((1, seq, hw), lambda b, h, i: (b, 0, h)),
                  vec(HEAD_DIM), vec(HEAD_DIM), vec(HEAD_DIM), vec(HEAD_DIM), vec(hw)],
        out_specs=pl.BlockSpec((1, tq, hw), lambda b, h, i: (b, i, h)),
        out_shape=jax.ShapeDtypeStruct((bsz, seq, heads * hw), BF16),
        scratch_shapes=[pltpu.VMEM((2, tq, 1), F32), pltpu.VMEM((2, tq, 1), F32), pltpu.VMEM((2, tq, hw), F32)],
        compiler_params=_cparams(("parallel", "parallel", "arbitrary")),
        name="diff_attention",
    )(q, k, v, lq1, lk1, lq2, lk2, subln_g)


def _sample_mix_body(zab_ref, st_ref, wdw_ref, bdw_ref, clg_ref, clb_ref, slg_ref, slb_ref, w00_ref, b0_ref,
                     ya_ref, st_out_ref, yb_ref, chv_ref, *, c):
    zab = zab_ref[...]
    g = zab[:, 0:c] * _sigmoid(zab[:, c:2 * c])
    acc = bdw_ref[...] + wdw_ref[CONV_WIDTH - 1:CONV_WIDTH, :] * g
    for j in range(CONV_WIDTH - 1):
        acc = acc + wdw_ref[j:j + 1, :] * st_ref[j]
        if j >= 1:
            st_out_ref[j - 1] = st_ref[j]
    st_out_ref[CONV_WIDTH - 2] = g
    y = _layernorm(acc, clg_ref[...], clb_ref[...])
    ya_ref[...] = y * _sigmoid(y)
    zu = _gelu_tanh(zab[:, 2 * c:3 * c])
    v = _layernorm(_gelu_tanh(zab[:, 3 * c:4 * c]), slg_ref[...], slb_ref[...])
    chv_ref[...] = v
    yb_ref[...] = zu * (w00_ref[...] * v + b0_ref[...])


def _sample_mixers(zab, st, w_dw, b_dw, clg, clb, slg, slb, w00, b0, layer, *, c):
    db = zab.shape[0]
    vec = lambda: pl.BlockSpec((None, 1, c), lambda i: (layer, 0, 0))
    return pl.pallas_call(
        functools.partial(_sample_mix_body, c=c),
        grid=(1,),
        in_specs=[pl.BlockSpec((db, 4 * c), lambda i: (0, 0)),
                  pl.BlockSpec((None, CONV_WIDTH - 1, db, c), lambda i: (layer, 0, 0, 0)),
                  pl.BlockSpec((None, CONV_WIDTH, c), lambda i: (layer, 0, 0)),
                  vec(), vec(), vec(), vec(), vec(), vec(), vec()],
        out_specs=[pl.BlockSpec((db, c), lambda i: (0, 0)),
                   pl.BlockSpec((CONV_WIDTH - 1, db, c), lambda i: (0, 0, 0)),
                   pl.BlockSpec((db, c), lambda i: (0, 0)),
                   pl.BlockSpec((db, c), lambda i: (0, 0))],
        out_shape=[jax.ShapeDtypeStruct((db, c), F32),
                   jax.ShapeDtypeStruct((CONV_WIDTH - 1, db, c), F32),
                   jax.ShapeDtypeStruct((db, c), F32),
                   jax.ShapeDtypeStruct((db, c), F32)],
        compiler_params=_cparams(("arbitrary",)),
        name="sample_mixers",
    )(zab, st, w_dw, b_dw, clg, clb, slg, slb, w00, b0)


def _paged_body(pt_ref, q_ref, kn_ref, vn_ref, kc_ref, vc_ref, lq1, lk1, lq2, lk2, g_ref, o_ref,
                m_sc, l_sc, acc_sc, *, heads, scale, lam_init):
    p = pl.program_id(1)
    hw = 2 * HEAD_DIM
    ones = jnp.ones((HEAD_DIM, LANES), BF16)

    @pl.when(p == 0)
    def _():
        m_sc[...] = jnp.full(m_sc.shape, -jnp.inf, F32)
        l_sc[...] = jnp.zeros(l_sc.shape, F32)
        acc_sc[...] = jnp.zeros(acc_sc.shape, F32)

    q = q_ref[...]

    def update(c, s, v):
        m_old = m_sc[c]
        m_new = jnp.maximum(m_old, jnp.max(s, axis=0))
        alpha = jnp.exp(m_old - m_new)
        e = jnp.exp(s - m_new[None])
        l_sc[c] = alpha * l_sc[c] + jnp.sum(e, axis=0)
        pv = jnp.sum(jnp.concatenate([e, e], axis=-1) * v, axis=0)
        acc_sc[c] = jnp.concatenate([alpha, alpha], axis=-1) * acc_sc[c] + pv
        m_sc[c] = m_new

    v = vc_ref[...]
    for c in range(2):
        kc = kc_ref[pl.ds(c, PAGE_SIZE * heads, stride=2), :]
        qc = q[:, c * HEAD_DIM:(c + 1) * HEAD_DIM]
        prod = kc.reshape(PAGE_SIZE, heads, HEAD_DIM) * qc[None]
        s = jnp.dot(prod.reshape(PAGE_SIZE * heads, HEAD_DIM).astype(BF16), ones, preferred_element_type=F32) * scale
        update(c, s.reshape(PAGE_SIZE, heads, LANES), v)

    @pl.when(p == pl.num_programs(1) - 1)
    def _():
        kn = kn_ref[...]
        vn = vn_ref[...]
        for c in range(2):
            sl = slice(c * HEAD_DIM, (c + 1) * HEAD_DIM)
            s = jnp.dot((kn[:, sl] * q[:, sl]).astype(BF16), ones, preferred_element_type=F32) * scale
            update(c, s[None], vn[None])
        lam = _lambda_full(lq1, lk1, lq2, lk2, lam_init)
        l1 = jnp.concatenate([l_sc[0], l_sc[0]], axis=-1)
        l2 = jnp.concatenate([l_sc[1], l_sc[1]], axis=-1)
        o = acc_sc[0] / l1 - lam * (acc_sc[1] / l2)
        o_ref[...] = _head_norm(o, g_ref[...], lam_init).astype(o_ref.dtype)


def _paged_attention(page_table, q, k_new, v_new, cache_k2, cache_v4, lq1, lk1, lq2, lk2, subln_g, layer, lam_init,
                     *, heads, n_pool):
    db, n_pages = page_table.shape
    hw = 2 * HEAD_DIM
    krows = PAGE_SIZE * heads * 2
    vec = lambda n: pl.BlockSpec((None, 1, n), lambda b, p, pt: (layer, 0, 0))
    tok = lambda: pl.BlockSpec((None, heads, hw), lambda b, p, pt: (b, 0, 0))
    body = functools.partial(_paged_body, heads=heads, scale=HEAD_DIM ** -0.5, lam_init=lam_init)
    grid_spec = pltpu.PrefetchScalarGridSpec(
        num_scalar_prefetch=1,
        grid=(db, n_pages),
        in_specs=[tok(), tok(), tok(),
                  pl.BlockSpec((krows, HEAD_DIM), lambda b, p, pt: (layer * n_pool + pt[b, p], 0)),
                  pl.BlockSpec((None, PAGE_SIZE, heads, hw), lambda b, p, pt: (layer * n_pool + pt[b, p], 0, 0, 0)),
                  vec(HEAD_DIM), vec(HEAD_DIM), vec(HEAD_DIM), vec(HEAD_DIM), vec(hw)],
        out_specs=tok(),
        scratch_shapes=[pltpu.VMEM((2, heads, LANES), F32), pltpu.VMEM((2, heads, LANES), F32),
                        pltpu.VMEM((2, heads, hw), F32)],
    )
    return pl.pallas_call(
        body,
        grid_spec=grid_spec,
        out_shape=jax.ShapeDtypeStruct((db, heads, hw), F32),
        compiler_params=_cparams(("parallel", "arbitrary")),
        name="paged_attention",
    )(page_table, q, k_new, v_new, cache_k2, cache_v4, lq1, lk1, lq2, lk2, subln_g)


def _rope_tables(pos):
    half = ROT_DIM // 2
    inv = jnp.power(ROPE_THETA, -(jnp.arange(half, dtype=F32) * (2.0 / ROT_DIM)))
    ang = pos.astype(F32)[:, None] * inv[None, :]
    cos, sin = jnp.cos(ang), jnp.sin(ang)
    n = pos.shape[0]
    pad = HEAD_DIM - ROT_DIM
    a = jnp.concatenate([cos, cos, jnp.ones((n, pad), F32)], axis=1)
    bm = jnp.concatenate([-sin, jnp.zeros((n, HEAD_DIM - half), F32)], axis=1)
    bp = jnp.concatenate([jnp.zeros((n, half), F32), sin, jnp.zeros((n, pad), F32)], axis=1)
    return a, bm, bp


def kernel(x_prompt, x_sample, cache_k, cache_v, state_conv, page_table, norm1_g, w_in, w_dw, b_dw, conv_ln_g,
           conv_ln_b, w_pw2, sgu_ln_g, sgu_ln_b, w_s, b_s, lambda_q1, lambda_k1, lambda_q2, lambda_k2, subln_g,
           w_o, norm2_g, w_gate, w_up, w_down, final_g):
    bsz, seq, d = x_prompt.shape
    db = x_sample.shape[0]
    depth = w_in.shape[0]
    c = w_pw2.shape[1]
    att_w = w_o.shape[1] - 2 * c
    heads = att_w // (2 * HEAD_DIM)
    groups = c // HEAD_DIM
    n_pool = cache_k.shape[1]
    n_pages = page_table.shape[1]
    past = n_pages * PAGE_SIZE
    d_ff = w_gate.shape[2]
    fp = -(-d_ff // 1024) * 1024
    m = bsz * seq
    assert x_sample.shape[1] == 1

    w_in_b = w_in.astype(BF16)
    w_pw2_b = w_pw2.astype(BF16)
    w_o_b = w_o.astype(BF16)
    w_gate_b = jnp.pad(w_gate.astype(BF16), ((0, 0), (0, 0), (0, fp - d_ff)))
    w_up_b = jnp.pad(w_up.astype(BF16), ((0, 0), (0, 0), (0, fp - d_ff)))
    w_down_b = jnp.pad(w_down.astype(BF16), ((0, 0), (0, fp - d_ff), (0, 0)))
    r3 = lambda a: a.reshape(depth, 1, a.shape[-1])
    norm1_3, norm2_3 = r3(norm1_g), r3(norm2_g)
    b_dw3, clg3, clb3, slg3, slb3 = r3(b_dw), r3(conv_ln_g), r3(conv_ln_b), r3(sgu_ln_g), r3(sgu_ln_b)
    lq1, lk1, lq2, lk2, sg3 = r3(lambda_q1), r3(lambda_k1), r3(lambda_q2), r3(lambda_k2), r3(subln_g)
    bs_full = jnp.repeat(jnp.swapaxes(b_s, 1, 2), HEAD_DIM, axis=2)
    w00 = r3(jnp.repeat(w_s[:, :, 0, 0], HEAD_DIM, axis=1))
    b0 = r3(jnp.repeat(b_s[:, :, 0], HEAD_DIM, axis=1))
    st_t = jnp.swapaxes(state_conv, 1, 2)
    cache_k2 = cache_k.reshape(depth * n_pool * PAGE_SIZE * heads * 2, HEAD_DIM)
    cache_v4 = cache_v.reshape(depth * n_pool, PAGE_SIZE, heads, 2 * HEAD_DIM)
    rope_p = _rope_tables(jnp.arange(seq, dtype=jnp.int32))
    rope_s = _rope_tables(jnp.full((db,), past, dtype=jnp.int32))

    hp = x_prompt.reshape(m, d)
    hs = x_sample.reshape(db, d)
    kp_l, vp_l, cp_l, ks_l, vs_l, cs_l, chs_l = [], [], [], [], [], [], []
    for l in range(depth):
        lam_init = 0.8 - 0.6 * math.exp(-0.3 * l)
        inproj = functools.partial(_matmul, w=w_in_b, layer=l, k_sizes=(d,))
        h = _rmsnorm(hp, norm1_g[l], BF16)
        (zab,) = inproj([h], n_off=0, n=4 * c, out_dtypes=(BF16,), name="inproj_mix")
        (q_b,) = inproj([h], n_off=4 * c, n=att_w, out_dtypes=(BF16,), rope=rope_p, rope_rows=seq, name="inproj_q")
        k_f, k_b = inproj([h], n_off=4 * c + att_w, n=att_w, out_dtypes=(F32, BF16), rope=rope_p, rope_rows=seq,
                          name="inproj_k")
        v_f, v_b = inproj([h], n_off=4 * c + 2 * att_w, n=att_w, out_dtypes=(F32, BF16), name="inproj_v")
        ya, conv_tail = _conv_mixer(zab.reshape(bsz, seq, 4 * c), w_dw, b_dw3, clg3, clb3, w_pw2_b, l,
                                    bsz=bsz, seq=seq, c=c)
        yb = _chunk_mixer(zab, slg3, slb3, w_s, bs_full, l, c=c)
        yc = _attention(q_b.reshape(bsz, seq, att_w), k_b.reshape(bsz, seq, att_w), v_b.reshape(bsz, seq, att_w),
                        lq1, lk1, lq2, lk2, sg3, l, lam_init, heads=heads)
        (hp,) = _matmul([ya.reshape(m, c), yb, yc.reshape(m, att_w)], w_o_b, l, k_sizes=(c, c, att_w), n_off=0, n=d,
                        out_dtypes=(F32,), res=hp, name="outproj")
        h2 = _rmsnorm(hp, norm2_g[l], BF16)
        hp = _down(_gateup(h2, w_gate_b, w_up_b, l), w_down_b, hp, l)
        kp_l.append(k_f.reshape(bsz, seq, heads, 2, HEAD_DIM))
        vp_l.append(v_f.reshape(bsz, seq, heads, 2 * HEAD_DIM))
        cp_l.append(conv_tail)
        hsn = _rmsnorm(hs, norm1_g[l], F32)
        (zab_s,) = inproj([hsn], n_off=0, n=4 * c, out_dtypes=(F32,), name="s_inproj_mix")
        (q_s,) = inproj([hsn], n_off=4 * c, n=att_w, out_dtypes=(F32,), rope=rope_s, rope_rows=db, name="s_inproj_q")
        (k_s,) = inproj([hsn], n_off=4 * c + att_w, n=att_w, out_dtypes=(F32,), rope=rope_s, rope_rows=db,
                        name="s_inproj_k")
        (v_s,) = inproj([hsn], n_off=4 * c + 2 * att_w, n=att_w, out_dtypes=(F32,), name="s_inproj_v")
        ya_pre, st_new, yb_s, chv = _sample_mixers(zab_s, st_t, w_dw, b_dw3, clg3, clb3, slg3, slb3, w00, b0, l, c=c)
        (ya_s,) = _matmul([ya_pre], w_pw2_b, l, k_sizes=(c,), n_off=0, n=c, out_dtypes=(F32,), name="s_pw2")
        hw = 2 * HEAD_DIM
        yc_s = _paged_attention(page_table, q_s.reshape(db, heads, hw), k_s.reshape(db, heads, hw),
                                v_s.reshape(db, heads, hw), cache_k2, cache_v4, lq1, lk1, lq2, lk2, sg3, l, lam_init,
                                heads=heads, n_pool=n_pool)
        (hs,) = _matmul([ya_s, yb_s, yc_s.reshape(db, att_w)], w_o_b, l, k_sizes=(c, c, att_w), n_off=0, n=d,
                        out_dtypes=(F32,), res=hs, name="s_outproj")
        hs2 = _rmsnorm(hs, norm2_g[l], F32)
        hs = _down(_gateup(hs2, w_gate_b, w_up_b, l), w_down_b, hs, l)
        ks_l.append(k_s.reshape(db, 1, heads, 2, HEAD_DIM))
        vs_l.append(v_s.reshape(db, 1, heads, 2 * HEAD_DIM))
        cs_l.append(jnp.swapaxes(st_new, 0, 1))
        chs_l.append(chv.reshape(db, 1, c))
    y_prompt = _rmsnorm(hp, final_g, F32).reshape(bsz, seq, d)
    y_sample = _rmsnorm(hs, final_g, F32).reshape(db, 1, d)
    return (y_prompt, y_sample, jnp.stack(kp_l), jnp.stack(vp_l), jnp.stack(cp_l), jnp.stack(ks_l),
            jnp.stack(vs_l), jnp.stack(cs_l), jnp.stack(chs_l))
```

```python
import functools
import math

import jax
import jax.numpy as jnp
from jax import lax
from jax.experimental import pallas as pl
from jax.experimental.pallas import tpu as pltpu

F32 = jnp.float32
BF16 = jnp.bfloat16

HEAD_DIM = 128
CONV_WIDTH = 31
CHUNK = 128
PAGE_SIZE = 128
ROT_DIM = HEAD_DIM // 4
ROPE_THETA = 500000.0
LANES = 128
SUBLANES = 8
CONV_HALO = 32
VMEM_LIMIT = 60 * 1024 * 1024
LOG2E = math.log2(math.e)
PAGES_PER_STEP = 2


def _cparams(sem):
    return pltpu.CompilerParams(dimension_semantics=sem, vmem_limit_bytes=VMEM_LIMIT)


def _sigmoid(x):
    return 1.0 / (1.0 + jnp.exp(-x))


def _gelu_tanh(x):
    c = math.sqrt(2.0 / math.pi)
    return x * (0.5 * (1.0 + jnp.tanh(c * (x + 0.044715 * (x * x * x)))))


def _layernorm(x, g, b, eps=1e-5):
    mu = jnp.mean(x, axis=-1, keepdims=True)
    xc = x - mu
    var = jnp.mean(xc * xc, axis=-1, keepdims=True)
    return xc * lax.rsqrt(var + eps) * g + b


def _rmsnorm_body(x_ref, g_ref, o_ref, *, eps):
    x = x_ref[...]
    ms = jnp.mean(x * x, axis=-1, keepdims=True)
    o_ref[...] = (x * lax.rsqrt(ms + eps) * g_ref[...]).astype(o_ref.dtype)


def _rmsnorm(x, g, out_dtype, eps=1e-6):
    m, d = x.shape
    tm = min(256, m)
    return pl.pallas_call(
        functools.partial(_rmsnorm_body, eps=eps),
        grid=(m // tm,),
        in_specs=[pl.BlockSpec((tm, d), lambda i: (i, 0)),
                  pl.BlockSpec((1, d), lambda i: (0, 0))],
        out_specs=pl.BlockSpec((tm, d), lambda i: (i, 0)),
        out_shape=jax.ShapeDtypeStruct((m, d), out_dtype),
        compiler_params=_cparams(("parallel",)),
        name="rmsnorm",
    )(x, g.reshape(1, d))


def _rope_group(x, a, bm, bp):
    return x * a + pltpu.roll(x, LANES - ROT_DIM // 2, 1) * bm + pltpu.roll(x, ROT_DIM // 2, 1) * bp


def _finish(acc, res_ref, tabs, outs, scale, tn):
    if res_ref is not None:
        acc = res_ref[...] + acc
    if tabs is not None:
        a, bm, bp = (t[...] for t in tabs)
        for g in range(tn // LANES):
            sl = slice(g * LANES, (g + 1) * LANES)
            piece = _rope_group(acc[:, sl], a, bm, bp)
            if scale is not None:
                piece = piece * scale
            for o in outs:
                o[:, sl] = piece.astype(o.dtype)
    else:
        if scale is not None:
            acc = acc * scale
        for o in outs:
            o[...] = acc.astype(o.dtype)


def _dots(x_refs, w_refs):
    acc = None
    for x_ref, w_ref in zip(x_refs, w_refs):
        d = jnp.dot(x_ref[...].astype(BF16), w_ref[...], preferred_element_type=F32)
        acc = d if acc is None else acc + d
    return acc


def _proj_body(*refs, n_lhs, has_res, has_rope, n_out_p, n_out_s, tn, p_scale):
    i = pl.program_id(1)
    pos = 0
    take = lambda n: refs[pos:pos + n]
    xp = take(n_lhs); pos += n_lhs
    xs = take(n_lhs); pos += n_lhs
    ws = take(n_lhs); pos += n_lhs
    res_p = res_s = tabs_p = tabs_s = None
    if has_res:
        res_p, res_s = take(2); pos += 2
    if has_rope:
        tabs_p = take(3); pos += 3
        tabs_s = take(3); pos += 3
    outs_p = take(n_out_p); pos += n_out_p
    outs_s = take(n_out_s); pos += n_out_s
    wbs = take(n_lhs)

    @pl.when(i == 0)
    def _():
        for w_ref, wb in zip(ws, wbs):
            wb[...] = w_ref[...].astype(BF16)
        _finish(_dots(xs, wbs), res_s, tabs_s, outs_s, None, tn)

    _finish(_dots(xp, wbs), res_p, tabs_p, outs_p, p_scale, tn)


def _proj(xp, xs, w, layer, *, k_sizes, n_off, n, out_p, out_s, res=None, rope=None, p_scale=None, name):
    m = xp[0].shape[0]
    db = xs[0].shape[0]
    tm = min(1024, m)
    tn = min(512, n)
    assert m % tm == 0 and n % tn == 0 and n_off % tn == 0
    in_specs, args = [], []
    for x, ks in zip(xp, k_sizes):
        in_specs.append(pl.BlockSpec((tm, ks), lambda j, i: (i, 0)))
        args.append(x)
    for x, ks in zip(xs, k_sizes):
        in_specs.append(pl.BlockSpec((db, ks), lambda j, i: (0, 0)))
        args.append(x)
    off = 0
    for ks in k_sizes:
        assert off % ks == 0
        in_specs.append(pl.BlockSpec((None, ks, tn), lambda j, i, kb=off // ks: (layer, kb, j + n_off // tn)))
        args.append(w)
        off += ks
    if res is not None:
        in_specs += [pl.BlockSpec((tm, tn), lambda j, i: (i, j)), pl.BlockSpec((db, tn), lambda j, i: (0, j))]
        args += list(res)
    if rope is not None:
        (tabs_p, period), tabs_s = rope
        assert period % tm == 0
        nrb = period // tm
        for t in tabs_p:
            in_specs.append(pl.BlockSpec((tm, LANES), lambda j, i: (i % nrb, 0)))
            args.append(t)
        for t in tabs_s:
            in_specs.append(pl.BlockSpec((db, LANES), lambda j, i: (0, 0)))
            args.append(t)
    body = functools.partial(_proj_body, n_lhs=len(xp), has_res=res is not None, has_rope=rope is not None,
                             n_out_p=len(out_p), n_out_s=len(out_s), tn=tn, p_scale=p_scale)
    outs = pl.pallas_call(
        body,
        grid=(n // tn, m // tm),
        in_specs=in_specs,
        out_specs=([pl.BlockSpec((tm, tn), lambda j, i: (i, j)) for _ in out_p]
                   + [pl.BlockSpec((db, tn), lambda j, i: (0, j)) for _ in out_s]),
        out_shape=([jax.ShapeDtypeStruct((m, n), dt) for dt in out_p]
                   + [jax.ShapeDtypeStruct((db, n), dt) for dt in out_s]),
        scratch_shapes=[pltpu.VMEM((ks, tn), BF16) for ks in k_sizes],
        compiler_params=_cparams(("parallel", "arbitrary")),
        name=name,
    )(*args)
    return outs[:len(out_p)], outs[len(out_p):]


def _mm_body(x_ref, w_ref, *rest, has_res):
    o_ref = rest[-1]
    acc = jnp.dot(x_ref[...].astype(BF16), w_ref[...], preferred_element_type=F32)
    if has_res:
        acc = rest[0][...] + acc
    o_ref[...] = acc.astype(o_ref.dtype)


def _matmul(x, w, layer, *, tm, tn, res=None, name):
    m, k = x.shape
    n = w.shape[2]
    assert m % tm == 0 and n % tn == 0
    in_specs = [pl.BlockSpec((tm, k), lambda i, j: (i, 0)),
                pl.BlockSpec((None, k, tn), lambda i, j: (layer, 0, j))]
    args = [x, w]
    if res is not None:
        in_specs.append(pl.BlockSpec((tm, tn), lambda i, j: (i, j)))
        args.append(res)
    return pl.pallas_call(
        functools.partial(_mm_body, has_res=res is not None),
        grid=(m // tm, n // tn),
        in_specs=in_specs,
        out_specs=pl.BlockSpec((tm, tn), lambda i, j: (i, j)),
        out_shape=jax.ShapeDtypeStruct((m, n), F32),
        compiler_params=_cparams(("parallel", "parallel")),
        name=name,
    )(*args)


def _swiglu(x, wg, wu):
    g = jnp.dot(x, wg, preferred_element_type=F32)
    u = jnp.dot(x, wu, preferred_element_type=F32)
    return g * _sigmoid(g) * u


def _gateup_body(xp_ref, xs_ref, wg_ref, wu_ref, op_ref, os_ref, wgb, wub):
    @pl.when(pl.program_id(1) == 0)
    def _():
        wgb[...] = wg_ref[...].astype(BF16)
        wub[...] = wu_ref[...].astype(BF16)
        os_ref[...] = _swiglu(xs_ref[...].astype(BF16), wgb[...], wub[...]).astype(os_ref.dtype)

    op_ref[...] = _swiglu(xp_ref[...], wgb[...], wub[...]).astype(op_ref.dtype)


def _gateup(xp, xs, wg, wu, layer):
    m, d = xp.shape
    db = xs.shape[0]
    f = wg.shape[2]
    tm = min(1024, m)
    tf = 256
    assert f % tf == 0 and m % tm == 0
    return pl.pallas_call(
        _gateup_body,
        grid=(f // tf, m // tm),
        in_specs=[pl.BlockSpec((tm, d), lambda j, i: (i, 0)),
                  pl.BlockSpec((db, d), lambda j, i: (0, 0)),
                  pl.BlockSpec((None, d, tf), lambda j, i: (layer, 0, j)),
                  pl.BlockSpec((None, d, tf), lambda j, i: (layer, 0, j))],
        out_specs=[pl.BlockSpec((tm, tf), lambda j, i: (i, j)),
                   pl.BlockSpec((db, tf), lambda j, i: (0, j))],
        out_shape=[jax.ShapeDtypeStruct((m, f), BF16), jax.ShapeDtypeStruct((db, f), BF16)],
        scratch_shapes=[pltpu.VMEM((d, tf), BF16), pltpu.VMEM((d, tf), BF16)],
        compiler_params=_cparams(("parallel", "arbitrary")),
        name="ffn_gateup",
    )(xp, xs, wg, wu)


def _conv_body(a_ref, gt_ref, ah_ref, gth_ref, wdw_ref, bdw_ref, lg_ref, lb_ref, wpw_ref,
               y_ref, tail_ref, buf, ybuf, *, tt, c, rows):
    t = pl.program_id(1)
    g = a_ref[0].astype(F32) * _sigmoid(gt_ref[0].astype(F32))
    gh = ah_ref[0].astype(F32) * _sigmoid(gth_ref[0].astype(F32))
    buf[0, 0:CONV_HALO, :] = jnp.where(t > 0, gh, 0.0)
    buf[0, CONV_HALO:CONV_HALO + tt, :] = g
    for s in range(1, SUBLANES):
        buf[s, 0:tt + CONV_HALO - SUBLANES, :] = buf[0, s:s + tt + CONV_HALO - SUBLANES, :]
    base = CONV_HALO - (CONV_WIDTH - 1)
    for r0 in range(0, tt, rows):
        for c0 in range(0, c, LANES):
            acc = jnp.broadcast_to(bdw_ref[:, c0:c0 + LANES], (rows, LANES))
            for j in range(CONV_WIDTH):
                s, a = (base + j) % SUBLANES, (base + j) // SUBLANES * SUBLANES
                acc = acc + buf[s, r0 + a:r0 + a + rows, c0:c0 + LANES] * wdw_ref[j:j + 1, c0:c0 + LANES]
            ybuf[r0:r0 + rows, c0:c0 + LANES] = acc
    y = _layernorm(ybuf[...], lg_ref[...], lb_ref[...])
    y = y * _sigmoid(y)
    y_ref[0] = jnp.dot(y.astype(BF16), wpw_ref[...], preferred_element_type=F32).astype(y_ref.dtype)

    @pl.when(t == pl.num_programs(1) - 1)
    def _():
        tail_ref[0] = buf[0, CONV_HALO + tt - (CONV_WIDTH - 1):CONV_HALO + tt, :]


def _conv_mixer(zab, w_dw, b_dw, ln_g, ln_b, w_pw2, layer, *, bsz, seq, c):
    tt = min(256, seq)
    rows = min(64, tt)
    hb = tt // CONV_HALO
    body = functools.partial(_conv_body, tt=tt, c=c, rows=rows)
    vec = lambda: pl.BlockSpec((None, 1, c), lambda b, t: (layer, 0, 0))
    return pl.pallas_call(
        body,
        grid=(bsz, seq // tt),
        in_specs=[pl.BlockSpec((1, tt, c), lambda b, t: (b, t, 0)),
                  pl.BlockSpec((1, tt, c), lambda b, t: (b, t, 1)),
                  pl.BlockSpec((1, CONV_HALO, c), lambda b, t: (b, jnp.maximum(t * hb - 1, 0), 0)),
                  pl.BlockSpec((1, CONV_HALO, c), lambda b, t: (b, jnp.maximum(t * hb - 1, 0), 1)),
                  pl.BlockSpec((None, CONV_WIDTH, c), lambda b, t: (layer, 0, 0)),
                  vec(), vec(), vec(),
                  pl.BlockSpec((None, c, c), lambda b, t: (layer, 0, 0))],
        out_specs=[pl.BlockSpec((1, tt, c), lambda b, t: (b, t, 0)),
                   pl.BlockSpec((1, CONV_WIDTH - 1, c), lambda b, t: (b, 0, 0))],
        out_shape=[jax.ShapeDtypeStruct((bsz, seq, c), BF16),
                   jax.ShapeDtypeStruct((bsz, CONV_WIDTH - 1, c), F32)],
        scratch_shapes=[pltpu.VMEM((SUBLANES, tt + CONV_HALO, c), F32), pltpu.VMEM((tt, c), F32)],
        compiler_params=_cparams(("parallel", "arbitrary")),
        name="conv_mixer",
    )(zab, zab, zab, zab, w_dw, b_dw, ln_g, ln_b, w_pw2)


def _chunk_body(u_ref, v_ref, lg_ref, lb_ref, ws_ref, bs_ref, o_ref, *, tt, groups):
    zu = _gelu_tanh(u_ref[...].astype(F32))
    v = _layernorm(_gelu_tanh(v_ref[...].astype(F32)), lg_ref[...], lb_ref[...]).astype(BF16)
    row = lax.broadcasted_iota(jnp.int32, (CHUNK, CHUNK), 0)
    col = lax.broadcasted_iota(jnp.int32, (CHUNK, CHUNK), 1)
    causal = col <= row
    bs = bs_ref[...]
    for g in range(groups):
        sl = slice(g * LANES, (g + 1) * LANES)
        wg = jnp.where(causal, ws_ref[g], 0.0).astype(BF16)
        for r0 in range(0, tt, CHUNK):
            mixed = jnp.dot(wg, v[r0:r0 + CHUNK, sl], preferred_element_type=F32) + bs[:, sl]
            o_ref[r0:r0 + CHUNK, sl] = (zu[r0:r0 + CHUNK, sl] * mixed).astype(o_ref.dtype)


def _chunk_mixer(zab, ln_g, ln_b, w_s, bs_full, layer, *, c):
    m = zab.shape[0]
    tt = min(256, m)
    groups = c // HEAD_DIM
    vec = lambda: pl.BlockSpec((None, 1, c), lambda i: (layer, 0, 0))
    return pl.pallas_call(
        functools.partial(_chunk_body, tt=tt, groups=groups),
        grid=(m // tt,),
        in_specs=[pl.BlockSpec((tt, c), lambda i: (i, 2)),
                  pl.BlockSpec((tt, c), lambda i: (i, 3)),
                  vec(), vec(),
                  pl.BlockSpec((None, groups, CHUNK, CHUNK), lambda i: (layer, 0, 0, 0)),
                  pl.BlockSpec((None, CHUNK, c), lambda i: (layer, 0, 0))],
        out_specs=pl.BlockSpec((tt, c), lambda i: (i, 0)),
        out_shape=jax.ShapeDtypeStruct((m, c), BF16),
        compiler_params=_cparams(("parallel",)),
        name="chunk_mixer",
    )(zab, zab, ln_g, ln_b, w_s, bs_full)


def _lambda_full(lq1, lk1, lq2, lk2, lam_init):
    s1 = jnp.sum(lq1[...] * lk1[...], axis=-1, keepdims=True)
    s2 = jnp.sum(lq2[...] * lk2[...], axis=-1, keepdims=True)
    return jnp.exp(s1) - jnp.exp(s2) + lam_init


def _head_norm(o, g, lam_init):
    return o * lax.rsqrt(jnp.mean(o * o, axis=-1, keepdims=True) + 1e-5) * g * (1.0 - lam_init)


def _attn_body(q_ref, k_ref, v_ref, lq1, lk1, lq2, lk2, g_ref, o_ref, m_sc, l_sc, acc_sc, *, tq, lam_init):
    qi = pl.program_id(2)
    q = q_ref[0]
    row = lax.broadcasted_iota(jnp.int32, (tq, tq), 0)
    col = lax.broadcasted_iota(jnp.int32, (tq, tq), 1)
    causal = col <= row

    def block(off, first):
        kb = k_ref[0, pl.ds(off, tq), :]
        vb = v_ref[0, pl.ds(off, tq), :]
        for c in range(2):
            sl = slice(c * HEAD_DIM, (c + 1) * HEAD_DIM)
            s = lax.dot_general(q[:, sl], kb[:, sl], (((1,), (1,)), ((), ())), preferred_element_type=F32)
            if first:
                s = jnp.where(causal, s, -jnp.inf)
            slabs = [s[:, g * LANES:(g + 1) * LANES] for g in range(tq // LANES)]
            part = functools.reduce(jnp.maximum, slabs)
            m_new = jnp.broadcast_to(jnp.max(part, axis=1, keepdims=True), (tq, LANES))
            if not first:
                m_old = m_sc[c]
                m_new = jnp.maximum(m_old, m_new)
                alpha = jnp.exp2(m_old - m_new)
            ps = [jnp.exp2(x - m_new) for x in slabs]
            lsum = functools.reduce(jnp.add, ps)
            pv = jnp.dot(jnp.concatenate(ps, axis=1).astype(BF16), vb, preferred_element_type=F32)
            if first:
                l_sc[c] = lsum
                acc_sc[c] = pv
            else:
                l_sc[c] = alpha * l_sc[c] + lsum
                acc_sc[c] = jnp.concatenate([alpha, alpha], axis=1) * acc_sc[c] + pv
            m_sc[c] = m_new

    block(pl.multiple_of(qi * tq, tq), True)

    def body(kj, carry):
        block(pl.multiple_of(kj * tq, tq), False)
        return carry

    lax.fori_loop(0, qi, body, 0)
    lam = _lambda_full(lq1, lk1, lq2, lk2, lam_init)
    l1 = jnp.sum(l_sc[0], axis=1, keepdims=True)
    l2 = jnp.sum(l_sc[1], axis=1, keepdims=True)
    o = acc_sc[0] / l1 - lam * (acc_sc[1] / l2)
    o_ref[0] = _head_norm(o, g_ref[...], lam_init).astype(o_ref.dtype)


def _attention(q, k, v, lq1, lk1, lq2, lk2, subln_g, layer, lam_init, *, heads):
    bsz, seq, _ = q.shape
    tq = min(512, seq)
    hw = 2 * HEAD_DIM
    vec = lambda n: pl.BlockSpec((None, 1, n), lambda b, h, i: (layer, 0, 0))
    body = functools.partial(_attn_body, tq=tq, lam_init=lam_init)
    return pl.pallas_call(
        body,
        grid=(bsz, heads, seq // tq),
        in_specs=[pl.BlockSpec((1, tq, hw), lambda b, h, i: (b, i, h)),
                  pl.BlockSpec((1, seq, hw), lambda b, h, i: (b, 0, h)),
                  pl.BlockSpec((1, seq, hw), lambda b, h, i: (b, 0, h)),
                  vec(HEAD_DIM), vec(HEAD_DIM), vec(HEAD_DIM), vec(HEAD_DIM), vec(hw)],
        out_specs=pl.BlockSpec((1, tq, hw), lambda b, h, i: (b, i, h)),
        out_shape=jax.ShapeDtypeStruct((bsz, seq, heads * hw), BF16),
        scratch_shapes=[pltpu.VMEM((2, tq, LANES), F32), pltpu.VMEM((2, tq, LANES), F32),
                        pltpu.VMEM((2, tq, hw), F32)],
        compiler_params=_cparams(("parallel", "parallel", "arbitrary")),
        name="diff_attention",
    )(q, k, v, lq1, lk1, lq2, lk2, subln_g)


def _sample_mix_body(zab_ref, st_ref, wdw_ref, bdw_ref, clg_ref, clb_ref, slg_ref, slb_ref, w00_ref, b0_ref,
                     ya_ref, st_out_ref, yb_ref, chv_ref, *, c):
    zab = zab_ref[...]
    g = zab[:, 0:c] * _sigmoid(zab[:, c:2 * c])
    acc = bdw_ref[...] + wdw_ref[CONV_WIDTH - 1:CONV_WIDTH, :] * g
    for j in range(CONV_WIDTH - 1):
        acc = acc + wdw_ref[j:j + 1, :] * st_ref[j]
        if j >= 1:
            st_out_ref[j - 1] = st_ref[j]
    st_out_ref[CONV_WIDTH - 2] = g
    y = _layernorm(acc, clg_ref[...], clb_ref[...])
    ya_ref[...] = y * _sigmoid(y)
    zu = _gelu_tanh(zab[:, 2 * c:3 * c])
    v = _layernorm(_gelu_tanh(zab[:, 3 * c:4 * c]), slg_ref[...], slb_ref[...])
    chv_ref[...] = v
    yb_ref[...] = zu * (w00_ref[...] * v + b0_ref[...])


def _sample_mixers(zab, st, w_dw, b_dw, clg, clb, slg, slb, w00, b0, layer, *, c):
    db = zab.shape[0]
    vec = lambda: pl.BlockSpec((None, 1, c), lambda i: (layer, 0, 0))
    return pl.pallas_call(
        functools.partial(_sample_mix_body, c=c),
        grid=(1,),
        in_specs=[pl.BlockSpec((db, 4 * c), lambda i: (0, 0)),
                  pl.BlockSpec((None, CONV_WIDTH - 1, db, c), lambda i: (layer, 0, 0, 0)),
                  pl.BlockSpec((None, CONV_WIDTH, c), lambda i: (layer, 0, 0)),
                  vec(), vec(), vec(), vec(), vec(), vec(), vec()],
        out_specs=[pl.BlockSpec((db, c), lambda i: (0, 0)),
                   pl.BlockSpec((CONV_WIDTH - 1, db, c), lambda i: (0, 0, 0)),
                   pl.BlockSpec((db, c), lambda i: (0, 0)),
                   pl.BlockSpec((db, c), lambda i: (0, 0))],
        out_shape=[jax.ShapeDtypeStruct((db, c), F32),
                   jax.ShapeDtypeStruct((CONV_WIDTH - 1, db, c), F32),
                   jax.ShapeDtypeStruct((db, c), F32),
                   jax.ShapeDtypeStruct((db, c), F32)],
        compiler_params=_cparams(("arbitrary",)),
        name="sample_mixers",
    )(zab, st, w_dw, b_dw, clg, clb, slg, slb, w00, b0)


def _paged_body(pt_ref, q_ref, kn_ref, vn_ref, *rest, heads, lam_init):
    npg = PAGES_PER_STEP
    kc_refs = rest[:npg]
    vc_refs = rest[npg:2 * npg]
    lq1, lk1, lq2, lk2, g_ref, o_ref, m_sc, l_sc, acc_sc = rest[2 * npg:]
    p = pl.program_id(1)
    ones = jnp.ones((HEAD_DIM, LANES), BF16)

    @pl.when(p == 0)
    def _():
        m_sc[...] = jnp.full(m_sc.shape, -jnp.inf, F32)
        l_sc[...] = jnp.zeros(l_sc.shape, F32)
        acc_sc[...] = jnp.zeros(acc_sc.shape, F32)

    q = q_ref[...] * (LOG2E * HEAD_DIM ** -0.5)

    def update(c, ss, vs):
        m_old = m_sc[c]
        m_new = m_old
        for s in ss:
            m_new = jnp.maximum(m_new, jnp.max(s, axis=0))
        alpha = jnp.exp2(m_old - m_new)
        l_new = alpha * l_sc[c]
        acc = jnp.concatenate([alpha, alpha], axis=-1) * acc_sc[c]
        for s, v in zip(ss, vs):
            e = jnp.exp2(s - m_new[None])
            l_new = l_new + jnp.sum(e, axis=0)
            acc = acc + jnp.sum(jnp.concatenate([e, e], axis=-1) * v, axis=0)
        l_sc[c] = l_new
        acc_sc[c] = acc
        m_sc[c] = m_new

    vs = [vc[...] for vc in vc_refs]
    for c in range(2):
        qc = q[:, c * HEAD_DIM:(c + 1) * HEAD_DIM]
        ss = []
        for kc_ref in kc_refs:
            kc = kc_ref[pl.ds(c, PAGE_SIZE * heads, stride=2), :]
            prod = kc.reshape(PAGE_SIZE, heads, HEAD_DIM) * qc[None]
            s = jnp.dot(prod.reshape(PAGE_SIZE * heads, HEAD_DIM).astype(BF16), ones, preferred_element_type=F32)
            ss.append(s.reshape(PAGE_SIZE, heads, LANES))
        update(c, ss, vs)

    @pl.when(p == pl.num_programs(1) - 1)
    def _():
        kn = kn_ref[...]
        vn = vn_ref[...]
        for c in range(2):
            sl = slice(c * HEAD_DIM, (c + 1) * HEAD_DIM)
            s = jnp.dot((kn[:, sl] * q[:, sl]).astype(BF16), ones, preferred_element_type=F32)
            update(c, [s[None]], [vn[None]])
        lam = _lambda_full(lq1, lk1, lq2, lk2, lam_init)
        l1 = jnp.concatenate([l_sc[0], l_sc[0]], axis=-1)
        l2 = jnp.concatenate([l_sc[1], l_sc[1]], axis=-1)
        o = acc_sc[0] / l1 - lam * (acc_sc[1] / l2)
        o_ref[...] = _head_norm(o, g_ref[...], lam_init).astype(o_ref.dtype)


def _paged_attention(page_table, q, k_new, v_new, cache_k2, cache_v4, lq1, lk1, lq2, lk2, subln_g, layer, lam_init,
                     *, heads, n_pool):
    db, n_pages = page_table.shape
    npg = PAGES_PER_STEP
    assert n_pages % npg == 0
    hw = 2 * HEAD_DIM
    krows = PAGE_SIZE * heads * 2
    vec = lambda n: pl.BlockSpec((None, 1, n), lambda b, p, pt: (layer, 0, 0))
    tok = lambda: pl.BlockSpec((None, heads, hw), lambda b, p, pt: (b, 0, 0))
    kspec = lambda r: pl.BlockSpec((krows, HEAD_DIM), lambda b, p, pt: (layer * n_pool + pt[b, p * npg + r], 0))
    vspec = lambda r: pl.BlockSpec((None, PAGE_SIZE, heads, hw),
                                   lambda b, p, pt: (layer * n_pool + pt[b, p * npg + r], 0, 0, 0))
    body = functools.partial(_paged_body, heads=heads, lam_init=lam_init)
    grid_spec = pltpu.PrefetchScalarGridSpec(
        num_scalar_prefetch=1,
        grid=(db, n_pages // npg),
        in_specs=([tok(), tok(), tok()] + [kspec(r) for r in range(npg)] + [vspec(r) for r in range(npg)]
                  + [vec(HEAD_DIM), vec(HEAD_DIM), vec(HEAD_DIM), vec(HEAD_DIM), vec(hw)]),
        out_specs=tok(),
        scratch_shapes=[pltpu.VMEM((2, heads, LANES), F32), pltpu.VMEM((2, heads, LANES), F32),
                        pltpu.VMEM((2, heads, hw), F32)],
    )
    return pl.pallas_call(
        body,
        grid_spec=grid_spec,
        out_shape=jax.ShapeDtypeStruct((db, heads, hw), F32),
        compiler_params=_cparams(("parallel", "arbitrary")),
        name="paged_attention",
    )(page_table, q, k_new, v_new, *([cache_k2] * npg), *([cache_v4] * npg), lq1, lk1, lq2, lk2, subln_g)


def _rope_tables(pos):
    half = ROT_DIM // 2
    inv = jnp.power(ROPE_THETA, -(jnp.arange(half, dtype=F32) * (2.0 / ROT_DIM)))
    ang = pos.astype(F32)[:, None] * inv[None, :]
    cos, sin = jnp.cos(ang), jnp.sin(ang)
    n = pos.shape[0]
    pad = HEAD_DIM - ROT_DIM
    a = jnp.concatenate([cos, cos, jnp.ones((n, pad), F32)], axis=1)
    bm = jnp.concatenate([-sin, jnp.zeros((n, HEAD_DIM - half), F32)], axis=1)
    bp = jnp.concatenate([jnp.zeros((n, half), F32), sin, jnp.zeros((n, pad), F32)], axis=1)
    return a, bm, bp


def kernel(x_prompt, x_sample, cache_k, cache_v, state_conv, page_table, norm1_g, w_in, w_dw, b_dw, conv_ln_g,
           conv_ln_b, w_pw2, sgu_ln_g, sgu_ln_b, w_s, b_s, lambda_q1, lambda_k1, lambda_q2, lambda_k2, subln_g,
           w_o, norm2_g, w_gate, w_up, w_down, final_g):
    bsz, seq, d = x_prompt.shape
    db = x_sample.shape[0]
    depth = w_in.shape[0]
    c = w_pw2.shape[1]
    att_w = w_o.shape[1] - 2 * c
    heads = att_w // (2 * HEAD_DIM)
    n_pool = cache_k.shape[1]
    past = page_table.shape[1] * PAGE_SIZE
    m = bsz * seq
    hw = 2 * HEAD_DIM
    assert x_sample.shape[1] == 1

    w_pw2_b = w_pw2.astype(BF16)
    w_down_b = w_down.astype(BF16)
    r3 = lambda a: a.reshape(depth, 1, a.shape[-1])
    b_dw3, clg3, clb3, slg3, slb3 = r3(b_dw), r3(conv_ln_g), r3(conv_ln_b), r3(sgu_ln_g), r3(sgu_ln_b)
    lq1, lk1, lq2, lk2, sg3 = r3(lambda_q1), r3(lambda_k1), r3(lambda_q2), r3(lambda_k2), r3(subln_g)
    bs_full = jnp.repeat(jnp.swapaxes(b_s, 1, 2), HEAD_DIM, axis=2)
    w00 = r3(jnp.repeat(w_s[:, :, 0, 0], HEAD_DIM, axis=1))
    b0 = r3(jnp.repeat(b_s[:, :, 0], HEAD_DIM, axis=1))
    st_t = jnp.swapaxes(state_conv, 1, 2)
    cache_k2 = cache_k.reshape(depth * n_pool * PAGE_SIZE * heads * 2, HEAD_DIM)
    cache_v4 = cache_v.reshape(depth * n_pool, PAGE_SIZE, heads, hw)
    rope = ((_rope_tables(jnp.arange(seq, dtype=jnp.int32)), seq),
            _rope_tables(jnp.full((db,), past, dtype=jnp.int32)))
    q_scale = LOG2E * HEAD_DIM ** -0.5

    hp = x_prompt.reshape(m, d)
    hs = x_sample.reshape(db, d)
    kp_l, vp_l, cp_l, ks_l, vs_l, cs_l, chs_l = [], [], [], [], [], [], []
    for l in range(depth):
        lam_init = 0.8 - 0.6 * math.exp(-0.3 * l)
        h = _rmsnorm(hp, norm1_g[l], BF16)
        hsn = _rmsnorm(hs, norm1_g[l], F32)
        inproj = functools.partial(_proj, [h], [hsn], w_in, l, k_sizes=(d,))
        (zab,), (zab_s,) = inproj(n_off=0, n=4 * c, out_p=(BF16,), out_s=(F32,), name="inproj_mix")
        (q_b,), (q_s,) = inproj(n_off=4 * c, n=att_w, out_p=(BF16,), out_s=(F32,), rope=rope, p_scale=q_scale,
                                name="inproj_q")
        (k_f, k_b), (k_s,) = inproj(n_off=4 * c + att_w, n=att_w, out_p=(F32, BF16), out_s=(F32,), rope=rope,
                                    name="inproj_k")
        (v_f, v_b), (v_s,) = inproj(n_off=4 * c + 2 * att_w, n=att_w, out_p=(F32, BF16), out_s=(F32,),
                                    name="inproj_v")
        ya, conv_tail = _conv_mixer(zab.reshape(bsz, seq, 4 * c), w_dw, b_dw3, clg3, clb3, w_pw2_b, l,
                                    bsz=bsz, seq=seq, c=c)
        yb = _chunk_mixer(zab, slg3, slb3, w_s, bs_full, l, c=c)
        yc = _attention(q_b.reshape(bsz, seq, att_w), k_b.reshape(bsz, seq, att_w), v_b.reshape(bsz, seq, att_w),
                        lq1, lk1, lq2, lk2, sg3, l, lam_init, heads=heads)
        ya_pre, st_new, yb_s, chv = _sample_mixers(zab_s, st_t, w_dw, b_dw3, clg3, clb3, slg3, slb3, w00, b0, l, c=c)
        ya_s = _matmul(ya_pre, w_pw2_b, l, tm=db, tn=c, name="s_pw2")
        yc_s = _paged_attention(page_table, q_s.reshape(db, heads, hw), k_s.reshape(db, heads, hw),
                                v_s.reshape(db, heads, hw), cache_k2, cache_v4, lq1, lk1, lq2, lk2, sg3, l, lam_init,
                                heads=heads, n_pool=n_pool)
        (hp,), (hs,) = _proj([ya.reshape(m, c), yb, yc.reshape(m, att_w)], [ya_s, yb_s, yc_s.reshape(db, att_w)],
                             w_o, l, k_sizes=(c, c, att_w), n_off=0, n=d, out_p=(F32,), out_s=(F32,),
                             res=(hp, hs), name="outproj")
        h2 = _rmsnorm(hp, norm2_g[l], BF16)
        hs2 = _rmsnorm(hs, norm2_g[l], F32)
        a_p, a_s = _gateup(h2, hs2, w_gate, w_up, l)
        hp = _matmul(a_p, w_down_b, l, tm=min(512, m), tn=min(512, d), res=hp, name="ffn_down")
        hs = _matmul(a_s, w_down_b, l, tm=db, tn=min(512, d), res=hs, name="s_ffn_down")
        kp_l.append(k_f.reshape(bsz, seq, heads, 2, HEAD_DIM))
        vp_l.append(v_f.reshape(bsz, seq, heads, hw))
        cp_l.append(conv_tail)
        ks_l.append(k_s.reshape(db, 1, heads, 2, HEAD_DIM))
        vs_l.append(v_s.reshape(db, 1, heads, hw))
        cs_l.append(jnp.swapaxes(st_new, 0, 1))
        chs_l.append(chv.reshape(db, 1, c))
    y_prompt = _rmsnorm(hp, final_g, F32).reshape(bsz, seq, d)
    y_sample = _rmsnorm(hs, final_g, F32).reshape(db, 1, d)
    return (y_prompt, y_sample, jnp.stack(kp_l), jnp.stack(vp_l), jnp.stack(cp_l), jnp.stack(ks_l),
            jnp.stack(vs_l), jnp.stack(cs_l), jnp.stack(chs_l))
```

```python
import functools
import math

import jax
import jax.numpy as jnp
from jax import lax
from jax.experimental import pallas as pl
from jax.experimental.pallas import tpu as pltpu

F32 = jnp.float32
BF16 = jnp.bfloat16

HEAD_DIM = 128
CONV_WIDTH = 31
CHUNK = 128
PAGE_SIZE = 128
ROT_DIM = HEAD_DIM // 4
ROPE_THETA = 500000.0
LANES = 128
SUBLANES = 8
CONV_HALO = 32
VMEM_LIMIT = 60 * 1024 * 1024
LOG2E = math.log2(math.e)
PAGES_PER_STEP = 4


def _cparams(sem):
    return pltpu.CompilerParams(dimension_semantics=sem, vmem_limit_bytes=VMEM_LIMIT)


def _sigmoid(x):
    return 1.0 / (1.0 + jnp.exp(-x))


def _gelu_tanh(x):
    c = math.sqrt(2.0 / math.pi)
    return x * (0.5 * (1.0 + jnp.tanh(c * (x + 0.044715 * (x * x * x)))))


def _layernorm(x, g, b, eps=1e-5):
    mu = jnp.mean(x, axis=-1, keepdims=True)
    xc = x - mu
    var = jnp.mean(xc * xc, axis=-1, keepdims=True)
    return xc * lax.rsqrt(var + eps) * g + b


def _rmsnorm_body(x_ref, g_ref, o_ref, *, eps):
    x = x_ref[...]
    ms = jnp.mean(x * x, axis=-1, keepdims=True)
    o_ref[...] = (x * lax.rsqrt(ms + eps) * g_ref[...]).astype(o_ref.dtype)


def _rmsnorm(x, g, out_dtype, eps=1e-6):
    m, d = x.shape
    tm = min(512, m)
    return pl.pallas_call(
        functools.partial(_rmsnorm_body, eps=eps),
        grid=(m // tm,),
        in_specs=[pl.BlockSpec((tm, d), lambda i: (i, 0)),
                  pl.BlockSpec((1, d), lambda i: (0, 0))],
        out_specs=pl.BlockSpec((tm, d), lambda i: (i, 0)),
        out_shape=jax.ShapeDtypeStruct((m, d), out_dtype),
        compiler_params=_cparams(("parallel",)),
        name="rmsnorm",
    )(x, g.reshape(1, d))


def _rope_group(x, a, bm, bp):
    return x * a + pltpu.roll(x, LANES - ROT_DIM // 2, 1) * bm + pltpu.roll(x, ROT_DIM // 2, 1) * bp


def _finish(acc, res_ref, tabs, outs, scale, tn):
    if res_ref is not None:
        acc = res_ref[...] + acc
    if tabs is not None:
        a, bm, bp = (t[...] for t in tabs)
        for g in range(tn // LANES):
            sl = slice(g * LANES, (g + 1) * LANES)
            piece = _rope_group(acc[:, sl], a, bm, bp)
            if scale is not None:
                piece = piece * scale
            for o in outs:
                o[:, sl] = piece.astype(o.dtype)
    else:
        if scale is not None:
            acc = acc * scale
        for o in outs:
            o[...] = acc.astype(o.dtype)


def _dots(x_refs, w_refs):
    acc = None
    for x_ref, w_ref in zip(x_refs, w_refs):
        d = jnp.dot(x_ref[...].astype(BF16), w_ref[...], preferred_element_type=F32)
        acc = d if acc is None else acc + d
    return acc


def _panel_dma(op, w_hbm, layer, rows, col0, tn, nj, n_valid, jj, dst, sem):
    tail = n_valid - (nj - 1) * tn

    def go(width):
        col = pl.multiple_of(col0 + jj * tn, LANES)
        cp = pltpu.make_async_copy(w_hbm.at[layer, pl.ds(rows[0], rows[1]), pl.ds(col, width)],
                                   dst.at[:, pl.ds(0, width)], sem)
        if op == "start":
            cp.start()
        else:
            cp.wait()

    if tail == tn:
        go(tn)
    else:
        pl.when(jj < nj - 1)(functools.partial(go, tn))
        pl.when(jj == nj - 1)(functools.partial(go, tail))


def _stage_panels(w_hbms, layer, rows, col0, tn, nj, n_valid, wfs, wbs, sems):
    j = pl.program_id(0)

    def dma(op, jj):
        for p, (w, r, wf) in enumerate(zip(w_hbms, rows, wfs)):
            _panel_dma(op, w, layer, r, col0, tn, nj, n_valid, jj, wf, sems.at[p])

    @pl.when(j == 0)
    def _():
        dma("start", j)

    dma("wait", j)
    for wf, wb in zip(wfs, wbs):
        wb[...] = wf[...].astype(BF16)

    @pl.when(j + 1 < nj)
    def _():
        dma("start", j + 1)


def _proj_body(*refs, n_lhs, has_res, has_rope, has_prev, n_out_p, n_out_s, tn, p_scale, layer, rows, col0, nj):
    i = pl.program_id(1)
    pos = 0
    take = lambda n: refs[pos:pos + n]
    xp = take(n_lhs); pos += n_lhs
    xs = take(n_lhs); pos += n_lhs
    w_hbm = refs[pos]; pos += 1
    res_p = res_s = tabs_p = tabs_s = None
    if has_res:
        res_p, res_s = take(2); pos += 2
    if has_rope:
        tabs_p = take(3); pos += 3
        tabs_s = take(3); pos += 3
    if has_prev:
        pos += 1
    outs_p = take(n_out_p); pos += n_out_p
    outs_s = take(n_out_s); pos += n_out_s
    wfs = take(n_lhs); pos += n_lhs
    wbs = take(n_lhs); pos += n_lhs
    sems = refs[pos]

    @pl.when(i == 0)
    def _():
        _stage_panels([w_hbm] * n_lhs, layer, rows, col0, tn, nj, nj * tn, wfs, wbs, sems)
        _finish(_dots(xs, wbs), res_s, tabs_s, outs_s, None, tn)

    _finish(_dots(xp, wbs), res_p, tabs_p, outs_p, p_scale, tn)


def _proj(xp, xs, w, layer, *, k_sizes, n_off, n, out_p, out_s, res=None, rope=None, p_scale=None, stacked=None,
          name):
    m = xp[0].shape[0]
    db = xs[0].shape[0]
    tm = min(512, m)
    tn = min(1024, n)
    assert m % tm == 0 and n % tn == 0 and n_off % LANES == 0
    in_specs, args = [], []
    for x, ks in zip(xp, k_sizes):
        in_specs.append(pl.BlockSpec((tm, ks), lambda j, i: (i, 0)))
        args.append(x)
    for x, ks in zip(xs, k_sizes):
        in_specs.append(pl.BlockSpec((db, ks), lambda j, i: (0, 0)))
        args.append(x)
    in_specs.append(pl.BlockSpec(memory_space=pl.ANY))
    args.append(w)
    rows, off = [], 0
    for ks in k_sizes:
        rows.append((off, ks))
        off += ks
    if res is not None:
        in_specs += [pl.BlockSpec((tm, tn), lambda j, i: (i, j)), pl.BlockSpec((db, tn), lambda j, i: (0, j))]
        args += list(res)
    if rope is not None:
        (tabs_p, period), tabs_s = rope
        assert period % tm == 0
        nrb = period // tm
        for t in tabs_p:
            in_specs.append(pl.BlockSpec((tm, LANES), lambda j, i: (i % nrb, 0)))
            args.append(t)
        for t in tabs_s:
            in_specs.append(pl.BlockSpec((db, LANES), lambda j, i: (0, 0)))
            args.append(t)
    p_specs = [pl.BlockSpec((tm, tn), lambda j, i: (i, j)) for _ in out_p]
    p_shapes = [jax.ShapeDtypeStruct((m, n), dt) for dt in out_p]
    aliases = {}
    if stacked is not None:
        depth, prev = stacked
        p_specs[0] = pl.BlockSpec((None, tm, tn), lambda j, i: (layer, i, j))
        p_shapes[0] = jax.ShapeDtypeStruct((depth, m, n), out_p[0])
        if prev is not None:
            in_specs.append(pl.BlockSpec(memory_space=pl.ANY))
            args.append(prev)
            aliases = {len(args) - 1: 0}
    body = functools.partial(_proj_body, n_lhs=len(xp), has_res=res is not None, has_rope=rope is not None,
                             has_prev=bool(aliases), n_out_p=len(out_p), n_out_s=len(out_s), tn=tn, p_scale=p_scale,
                             layer=layer, rows=tuple(rows), col0=n_off, nj=n // tn)
    outs = pl.pallas_call(
        body,
        grid=(n // tn, m // tm),
        in_specs=in_specs,
        out_specs=p_specs + [pl.BlockSpec((db, tn), lambda j, i: (0, j)) for _ in out_s],
        out_shape=p_shapes + [jax.ShapeDtypeStruct((db, n), dt) for dt in out_s],
        scratch_shapes=([pltpu.VMEM((ks, tn), F32) for ks in k_sizes] + [pltpu.VMEM((ks, tn), BF16) for ks in k_sizes]
                        + [pltpu.SemaphoreType.DMA((len(k_sizes),))]),
        input_output_aliases=aliases,
        compiler_params=_cparams(("arbitrary", "arbitrary")),
        name=name,
    )(*args)
    return outs[:len(out_p)], outs[len(out_p):]


def _mm_body(x_ref, w_ref, *rest, has_res):
    o_ref = rest[-1]
    acc = jnp.dot(x_ref[...].astype(BF16), w_ref[...], preferred_element_type=F32)
    if has_res:
        acc = rest[0][...] + acc
    o_ref[...] = acc.astype(o_ref.dtype)


def _matmul(x, w, layer, *, tm, tn, res=None, name):
    m, k = x.shape
    n = w.shape[2]
    assert m % tm == 0 and n % tn == 0
    in_specs = [pl.BlockSpec((tm, k), lambda i, j: (i, 0)),
                pl.BlockSpec((None, k, tn), lambda i, j: (layer, 0, j))]
    args = [x, w]
    if res is not None:
        in_specs.append(pl.BlockSpec((tm, tn), lambda i, j: (i, j)))
        args.append(res)
    return pl.pallas_call(
        functools.partial(_mm_body, has_res=res is not None),
        grid=(m // tm, n // tn),
        in_specs=in_specs,
        out_specs=pl.BlockSpec((tm, tn), lambda i, j: (i, j)),
        out_shape=jax.ShapeDtypeStruct((m, n), F32),
        compiler_params=_cparams(("parallel", "parallel")),
        name=name,
    )(*args)


def _swiglu(x, wg, wu):
    g = jnp.dot(x, wg, preferred_element_type=F32)
    u = jnp.dot(x, wu, preferred_element_type=F32)
    return g * _sigmoid(g) * u


def _gateup_body(xp_ref, xs_ref, wg_hbm, wu_hbm, op_ref, os_ref, wgf, wuf, wgb, wub, sems, *, layer, d, tf, nj, f):
    @pl.when(pl.program_id(1) == 0)
    def _():
        _stage_panels([wg_hbm, wu_hbm], layer, [(0, d), (0, d)], 0, tf, nj, f, [wgf, wuf], [wgb, wub], sems)
        os_ref[...] = _swiglu(xs_ref[...].astype(BF16), wgb[...], wub[...]).astype(os_ref.dtype)

    op_ref[...] = _swiglu(xp_ref[...], wgb[...], wub[...]).astype(op_ref.dtype)


def _gateup(xp, xs, wg, wu, layer):
    m, d = xp.shape
    db = xs.shape[0]
    f = wg.shape[2]
    tm = min(1024, m)
    tf = min(512, f)
    nj = pl.cdiv(f, tf)
    assert m % tm == 0 and f % LANES == 0 and (nj > 1 or f == tf)
    body = functools.partial(_gateup_body, layer=layer, d=d, tf=tf, nj=nj, f=f)
    return pl.pallas_call(
        body,
        grid=(nj, m // tm),
        in_specs=[pl.BlockSpec((tm, d), lambda j, i: (i, 0)),
                  pl.BlockSpec((db, d), lambda j, i: (0, 0)),
                  pl.BlockSpec(memory_space=pl.ANY),
                  pl.BlockSpec(memory_space=pl.ANY)],
        out_specs=[pl.BlockSpec((tm, tf), lambda j, i: (i, j)),
                   pl.BlockSpec((db, tf), lambda j, i: (0, j))],
        out_shape=[jax.ShapeDtypeStruct((m, f), BF16), jax.ShapeDtypeStruct((db, f), BF16)],
        scratch_shapes=[pltpu.VMEM((d, tf), F32), pltpu.VMEM((d, tf), F32),
                        pltpu.VMEM((d, tf), BF16), pltpu.VMEM((d, tf), BF16), pltpu.SemaphoreType.DMA((2,))],
        compiler_params=_cparams(("arbitrary", "arbitrary")),
        name="ffn_gateup",
    )(xp, xs, wg, wu)


def _conv_body(a_ref, gt_ref, ah_ref, gth_ref, wdw_ref, bdw_ref, lg_ref, lb_ref, wpw_ref,
               y_ref, tail_ref, buf, ybuf, *, tt, c, rows):
    t = pl.program_id(1)
    g = a_ref[0].astype(F32) * _sigmoid(gt_ref[0].astype(F32))
    gh = ah_ref[0].astype(F32) * _sigmoid(gth_ref[0].astype(F32))
    buf[0, 0:CONV_HALO, :] = jnp.where(t > 0, gh, 0.0)
    buf[0, CONV_HALO:CONV_HALO + tt, :] = g
    for s in range(1, SUBLANES):
        buf[s, 0:tt + CONV_HALO - SUBLANES, :] = buf[0, s:s + tt + CONV_HALO - SUBLANES, :]
    base = CONV_HALO - (CONV_WIDTH - 1)
    for r0 in range(0, tt, rows):
        for c0 in range(0, c, LANES):
            acc = jnp.broadcast_to(bdw_ref[:, c0:c0 + LANES], (rows, LANES))
            for j in range(CONV_WIDTH):
                s, a = (base + j) % SUBLANES, (base + j) // SUBLANES * SUBLANES
                acc = acc + buf[s, r0 + a:r0 + a + rows, c0:c0 + LANES] * wdw_ref[j:j + 1, c0:c0 + LANES]
            ybuf[r0:r0 + rows, c0:c0 + LANES] = acc
    y = _layernorm(ybuf[...], lg_ref[...], lb_ref[...])
    y = y * _sigmoid(y)
    y_ref[0] = jnp.dot(y.astype(BF16), wpw_ref[...], preferred_element_type=F32).astype(y_ref.dtype)

    @pl.when(t == pl.num_programs(1) - 1)
    def _():
        tail_ref[0] = buf[0, CONV_HALO + tt - (CONV_WIDTH - 1):CONV_HALO + tt, :]


def _conv_mixer(zab, w_dw, b_dw, ln_g, ln_b, w_pw2, layer, *, bsz, seq, c):
    tt = min(256, seq)
    rows = min(64, tt)
    hb = tt // CONV_HALO
    body = functools.partial(_conv_body, tt=tt, c=c, rows=rows)
    vec = lambda: pl.BlockSpec((None, 1, c), lambda b, t: (layer, 0, 0))
    return pl.pallas_call(
        body,
        grid=(bsz, seq // tt),
        in_specs=[pl.BlockSpec((1, tt, c), lambda b, t: (b, t, 0)),
                  pl.BlockSpec((1, tt, c), lambda b, t: (b, t, 1)),
                  pl.BlockSpec((1, CONV_HALO, c), lambda b, t: (b, jnp.maximum(t * hb - 1, 0), 0)),
                  pl.BlockSpec((1, CONV_HALO, c), lambda b, t: (b, jnp.maximum(t * hb - 1, 0), 1)),
                  pl.BlockSpec((None, CONV_WIDTH, c), lambda b, t: (layer, 0, 0)),
                  vec(), vec(), vec(),
                  pl.BlockSpec((None, c, c), lambda b, t: (layer, 0, 0))],
        out_specs=[pl.BlockSpec((1, tt, c), lambda b, t: (b, t, 0)),
                   pl.BlockSpec((1, CONV_WIDTH - 1, c), lambda b, t: (b, 0, 0))],
        out_shape=[jax.ShapeDtypeStruct((bsz, seq, c), BF16),
                   jax.ShapeDtypeStruct((bsz, CONV_WIDTH - 1, c), F32)],
        scratch_shapes=[pltpu.VMEM((SUBLANES, tt + CONV_HALO, c), F32), pltpu.VMEM((tt, c), F32)],
        compiler_params=_cparams(("parallel", "arbitrary")),
        name="conv_mixer",
    )(zab, zab, zab, zab, w_dw, b_dw, ln_g, ln_b, w_pw2)


def _chunk_body(u_ref, v_ref, lg_ref, lb_ref, ws_ref, bs_ref, o_ref, *, tt, groups):
    zu = _gelu_tanh(u_ref[...].astype(F32))
    v = _layernorm(_gelu_tanh(v_ref[...].astype(F32)), lg_ref[...], lb_ref[...]).astype(BF16)
    row = lax.broadcasted_iota(jnp.int32, (CHUNK, CHUNK), 0)
    col = lax.broadcasted_iota(jnp.int32, (CHUNK, CHUNK), 1)
    causal = col <= row
    bs = bs_ref[...]
    for g in range(groups):
        sl = slice(g * LANES, (g + 1) * LANES)
        wg = jnp.where(causal, ws_ref[g], 0.0).astype(BF16)
        for r0 in range(0, tt, CHUNK):
            mixed = jnp.dot(wg, v[r0:r0 + CHUNK, sl], preferred_element_type=F32) + bs[:, sl]
            o_ref[r0:r0 + CHUNK, sl] = (zu[r0:r0 + CHUNK, sl] * mixed).astype(o_ref.dtype)


def _chunk_mixer(zab, ln_g, ln_b, w_s, bs_full, layer, *, c):
    m = zab.shape[0]
    tt = min(256, m)
    groups = c // HEAD_DIM
    vec = lambda: pl.BlockSpec((None, 1, c), lambda i: (layer, 0, 0))
    return pl.pallas_call(
        functools.partial(_chunk_body, tt=tt, groups=groups),
        grid=(m // tt,),
        in_specs=[pl.BlockSpec((tt, c), lambda i: (i, 2)),
                  pl.BlockSpec((tt, c), lambda i: (i, 3)),
                  vec(), vec(),
                  pl.BlockSpec((None, groups, CHUNK, CHUNK), lambda i: (layer, 0, 0, 0)),
                  pl.BlockSpec((None, CHUNK, c), lambda i: (layer, 0, 0))],
        out_specs=pl.BlockSpec((tt, c), lambda i: (i, 0)),
        out_shape=jax.ShapeDtypeStruct((m, c), BF16),
        compiler_params=_cparams(("parallel",)),
        name="chunk_mixer",
    )(zab, zab, ln_g, ln_b, w_s, bs_full)


def _lambda_full(lq1, lk1, lq2, lk2, lam_init):
    s1 = jnp.sum(lq1[...] * lk1[...], axis=-1, keepdims=True)
    s2 = jnp.sum(lq2[...] * lk2[...], axis=-1, keepdims=True)
    return jnp.exp(s1) - jnp.exp(s2) + lam_init


def _head_norm(o, g, lam_init):
    return o * lax.rsqrt(jnp.mean(o * o, axis=-1, keepdims=True) + 1e-5) * g * (1.0 - lam_init)


def _attn_body(q_ref, k_ref, v_ref, lq1, lk1, lq2, lk2, g_ref, o_ref, m_sc, l_sc, acc_sc, *, tq, lam_init):
    qi = pl.program_id(2)
    q = q_ref[0]
    row = lax.broadcasted_iota(jnp.int32, (tq, tq), 0)
    col = lax.broadcasted_iota(jnp.int32, (tq, tq), 1)
    causal = col <= row

    def block(off, first):
        kb = k_ref[0, pl.ds(off, tq), :]
        vb = v_ref[0, pl.ds(off, tq), :]
        for c in range(2):
            sl = slice(c * HEAD_DIM, (c + 1) * HEAD_DIM)
            s = lax.dot_general(q[:, sl], kb[:, sl], (((1,), (1,)), ((), ())), preferred_element_type=F32)
            if first:
                s = jnp.where(causal, s, -jnp.inf)
            slabs = [s[:, g * LANES:(g + 1) * LANES] for g in range(tq // LANES)]
            part = functools.reduce(jnp.maximum, slabs)
            m_new = jnp.broadcast_to(jnp.max(part, axis=1, keepdims=True), (tq, LANES))
            if not first:
                m_old = m_sc[c]
                m_new = jnp.maximum(m_old, m_new)
                alpha = jnp.exp2(m_old - m_new)
            ps = [jnp.exp2(x - m_new) for x in slabs]
            lsum = functools.reduce(jnp.add, ps)
            pv = jnp.dot(jnp.concatenate(ps, axis=1).astype(BF16), vb, preferred_element_type=F32)
            if first:
                l_sc[c] = lsum
                acc_sc[c] = pv
            else:
                l_sc[c] = alpha * l_sc[c] + lsum
                acc_sc[c] = jnp.concatenate([alpha, alpha], axis=1) * acc_sc[c] + pv
            m_sc[c] = m_new

    block(pl.multiple_of(qi * tq, tq), True)

    def body(kj, carry):
        block(pl.multiple_of(kj * tq, tq), False)
        return carry

    lax.fori_loop(0, qi, body, 0)
    lam = _lambda_full(lq1, lk1, lq2, lk2, lam_init)
    l1 = jnp.sum(l_sc[0], axis=1, keepdims=True)
    l2 = jnp.sum(l_sc[1], axis=1, keepdims=True)
    o = acc_sc[0] / l1 - lam * (acc_sc[1] / l2)
    o_ref[0] = _head_norm(o, g_ref[...], lam_init).astype(o_ref.dtype)


def _attention(q, k, v, lq1, lk1, lq2, lk2, subln_g, layer, lam_init, *, heads):
    bsz, seq, _ = q.shape
    tq = min(512, seq)
    hw = 2 * HEAD_DIM
    vec = lambda n: pl.BlockSpec((None, 1, n), lambda b, h, i: (layer, 0, 0))
    body = functools.partial(_attn_body, tq=tq, lam_init=lam_init)
    return pl.pallas_call(
        body,
        grid=(bsz, heads, seq // tq),
        in_specs=[pl.BlockSpec((1, tq, hw), lambda b, h, i: (b, i, h)),
                  pl.BlockSpec((1, seq, hw), lambda b, h, i: (b, 0, h)),
                  pl.BlockSpec((1, seq, hw), lambda b, h, i: (b, 0, h)),
                  vec(HEAD_DIM), vec(HEAD_DIM), vec(HEAD_DIM), vec(HEAD_DIM), vec(hw)],
        out_specs=pl.BlockSpec((1, tq, hw), lambda b, h, i: (b, i, h)),
        out_shape=jax.ShapeDtypeStruct((bsz, seq, heads * hw), BF16),
        scratch_shapes=[pltpu.VMEM((2, tq, LANES), F32), pltpu.VMEM((2, tq, LANES), F32),
                        pltpu.VMEM((2, tq, hw), F32)],
        compiler_params=_cparams(("parallel", "parallel", "arbitrary")),
        name="diff_attention",
    )(q, k, v, lq1, lk1, lq2, lk2, subln_g)


def _sample_mix_body(zab_ref, st_ref, wdw_ref, bdw_ref, clg_ref, clb_ref, slg_ref, slb_ref, w00_ref, b0_ref,
                     ya_ref, st_out_ref, yb_ref, chv_ref, *, c):
    zab = zab_ref[...]
    g = zab[:, 0:c] * _sigmoid(zab[:, c:2 * c])
    acc = bdw_ref[...] + wdw_ref[CONV_WIDTH - 1:CONV_WIDTH, :] * g
    for j in range(CONV_WIDTH - 1):
        acc = acc + wdw_ref[j:j + 1, :] * st_ref[j]
        if j >= 1:
            st_out_ref[j - 1] = st_ref[j]
    st_out_ref[CONV_WIDTH - 2] = g
    y = _layernorm(acc, clg_ref[...], clb_ref[...])
    ya_ref[...] = y * _sigmoid(y)
    zu = _gelu_tanh(zab[:, 2 * c:3 * c])
    v = _layernorm(_gelu_tanh(zab[:, 3 * c:4 * c]), slg_ref[...], slb_ref[...])
    chv_ref[...] = v
    yb_ref[...] = zu * (w00_ref[...] * v + b0_ref[...])


def _sample_mixers(zab, st, w_dw, b_dw, clg, clb, slg, slb, w00, b0, layer, *, c):
    db = zab.shape[0]
    vec = lambda: pl.BlockSpec((None, 1, c), lambda i: (layer, 0, 0))
    return pl.pallas_call(
        functools.partial(_sample_mix_body, c=c),
        grid=(1,),
        in_specs=[pl.BlockSpec((db, 4 * c), lambda i: (0, 0)),
                  pl.BlockSpec((None, CONV_WIDTH - 1, db, c), lambda i: (layer, 0, 0, 0)),
                  pl.BlockSpec((None, CONV_WIDTH, c), lambda i: (layer, 0, 0)),
                  vec(), vec(), vec(), vec(), vec(), vec(), vec()],
        out_specs=[pl.BlockSpec((db, c), lambda i: (0, 0)),
                   pl.BlockSpec((CONV_WIDTH - 1, db, c), lambda i: (0, 0, 0)),
                   pl.BlockSpec((db, c), lambda i: (0, 0)),
                   pl.BlockSpec((db, c), lambda i: (0, 0))],
        out_shape=[jax.ShapeDtypeStruct((db, c), F32),
                   jax.ShapeDtypeStruct((CONV_WIDTH - 1, db, c), F32),
                   jax.ShapeDtypeStruct((db, c), F32),
                   jax.ShapeDtypeStruct((db, c), F32)],
        compiler_params=_cparams(("arbitrary",)),
        name="sample_mixers",
    )(zab, st, w_dw, b_dw, clg, clb, slg, slb, w00, b0)


def _paged_body(pt_ref, q_ref, kn_ref, vn_ref, *rest, heads, lam_init):
    npg = PAGES_PER_STEP
    kc_refs = rest[:npg]
    vc_refs = rest[npg:2 * npg]
    lq1, lk1, lq2, lk2, g_ref, o_ref, m_sc, l_sc, acc_sc = rest[2 * npg:]
    p = pl.program_id(1)
    ones = jnp.ones((HEAD_DIM, LANES), BF16)

    @pl.when(p == 0)
    def _():
        m_sc[...] = jnp.full(m_sc.shape, -jnp.inf, F32)
        l_sc[...] = jnp.zeros(l_sc.shape, F32)
        acc_sc[...] = jnp.zeros(acc_sc.shape, F32)

    q = q_ref[...] * (LOG2E * HEAD_DIM ** -0.5)

    def update(c, ss, vs):
        m_old = m_sc[c]
        m_new = m_old
        for s in ss:
            m_new = jnp.maximum(m_new, jnp.max(s, axis=0))
        alpha = jnp.exp2(m_old - m_new)
        l_new = alpha * l_sc[c]
        acc = jnp.concatenate([alpha, alpha], axis=-1) * acc_sc[c]
        for s, v in zip(ss, vs):
            e = jnp.exp2(s - m_new[None])
            l_new = l_new + jnp.sum(e, axis=0)
            acc = acc + jnp.sum(jnp.concatenate([e, e], axis=-1) * v, axis=0)
        l_sc[c] = l_new
        acc_sc[c] = acc
        m_sc[c] = m_new

    vs = [vc[...] for vc in vc_refs]
    for c in range(2):
        qc = q[:, c * HEAD_DIM:(c + 1) * HEAD_DIM]
        ss = []
        for kc_ref in kc_refs:
            kc = kc_ref[pl.ds(c, PAGE_SIZE * heads, stride=2), :]
            prod = kc.reshape(PAGE_SIZE, heads, HEAD_DIM) * qc[None]
            s = jnp.dot(prod.reshape(PAGE_SIZE * heads, HEAD_DIM).astype(BF16), ones, preferred_element_type=F32)
            ss.append(s.reshape(PAGE_SIZE, heads, LANES))
        update(c, ss, vs)

    @pl.when(p == pl.num_programs(1) - 1)
    def _():
        kn = kn_ref[...]
        vn = vn_ref[...]
        for c in range(2):
            sl = slice(c * HEAD_DIM, (c + 1) * HEAD_DIM)
            s = jnp.dot((kn[:, sl] * q[:, sl]).astype(BF16), ones, preferred_element_type=F32)
            update(c, [s[None]], [vn[None]])
        lam = _lambda_full(lq1, lk1, lq2, lk2, lam_init)
        l1 = jnp.concatenate([l_sc[0], l_sc[0]], axis=-1)
        l2 = jnp.concatenate([l_sc[1], l_sc[1]], axis=-1)
        o = acc_sc[0] / l1 - lam * (acc_sc[1] / l2)
        o_ref[...] = _head_norm(o, g_ref[...], lam_init).astype(o_ref.dtype)


def _paged_attention(page_table, q, k_new, v_new, cache_k2, cache_v4, lq1, lk1, lq2, lk2, subln_g, layer, lam_init,
                     *, heads, n_pool):
    db, n_pages = page_table.shape
    npg = PAGES_PER_STEP
    assert n_pages % npg == 0
    hw = 2 * HEAD_DIM
    krows = PAGE_SIZE * heads * 2
    vec = lambda n: pl.BlockSpec((None, 1, n), lambda b, p, pt: (layer, 0, 0))
    tok = lambda: pl.BlockSpec((None, heads, hw), lambda b, p, pt: (b, 0, 0))
    kspec = lambda r: pl.BlockSpec((krows, HEAD_DIM), lambda b, p, pt: (layer * n_pool + pt[b, p * npg + r], 0))
    vspec = lambda r: pl.BlockSpec((None, PAGE_SIZE, heads, hw),
                                   lambda b, p, pt: (layer * n_pool + pt[b, p * npg + r], 0, 0, 0))
    body = functools.partial(_paged_body, heads=heads, lam_init=lam_init)
    grid_spec = pltpu.PrefetchScalarGridSpec(
        num_scalar_prefetch=1,
        grid=(db, n_pages // npg),
        in_specs=([tok(), tok(), tok()] + [kspec(r) for r in range(npg)] + [vspec(r) for r in range(npg)]
                  + [vec(HEAD_DIM), vec(HEAD_DIM), vec(HEAD_DIM), vec(HEAD_DIM), vec(hw)]),
        out_specs=tok(),
        scratch_shapes=[pltpu.VMEM((2, heads, LANES), F32), pltpu.VMEM((2, heads, LANES), F32),
                        pltpu.VMEM((2, heads, hw), F32)],
    )
    return pl.pallas_call(
        body,
        grid_spec=grid_spec,
        out_shape=jax.ShapeDtypeStruct((db, heads, hw), F32),
        compiler_params=_cparams(("parallel", "arbitrary")),
        name="paged_attention",
    )(page_table, q, k_new, v_new, *([cache_k2] * npg), *([cache_v4] * npg), lq1, lk1, lq2, lk2, subln_g)


def _rope_tables(pos):
    half = ROT_DIM // 2
    inv = jnp.power(ROPE_THETA, -(jnp.arange(half, dtype=F32) * (2.0 / ROT_DIM)))
    ang = pos.astype(F32)[:, None] * inv[None, :]
    cos, sin = jnp.cos(ang), jnp.sin(ang)
    n = pos.shape[0]
    pad = HEAD_DIM - ROT_DIM
    a = jnp.concatenate([cos, cos, jnp.ones((n, pad), F32)], axis=1)
    bm = jnp.concatenate([-sin, jnp.zeros((n, HEAD_DIM - half), F32)], axis=1)
    bp = jnp.concatenate([jnp.zeros((n, half), F32), sin, jnp.zeros((n, pad), F32)], axis=1)
    return a, bm, bp


def kernel(x_prompt, x_sample, cache_k, cache_v, state_conv, page_table, norm1_g, w_in, w_dw, b_dw, conv_ln_g,
           conv_ln_b, w_pw2, sgu_ln_g, sgu_ln_b, w_s, b_s, lambda_q1, lambda_k1, lambda_q2, lambda_k2, subln_g,
           w_o, norm2_g, w_gate, w_up, w_down, final_g):
    bsz, seq, d = x_prompt.shape
    db = x_sample.shape[0]
    depth = w_in.shape[0]
    c = w_pw2.shape[1]
    att_w = w_o.shape[1] - 2 * c
    heads = att_w // (2 * HEAD_DIM)
    n_pool = cache_k.shape[1]
    past = page_table.shape[1] * PAGE_SIZE
    m = bsz * seq
    hw = 2 * HEAD_DIM
    assert x_sample.shape[1] == 1

    w_pw2_b = w_pw2.astype(BF16)
    w_down_b = w_down.astype(BF16)
    r3 = lambda a: a.reshape(depth, 1, a.shape[-1])
    b_dw3, clg3, clb3, slg3, slb3 = r3(b_dw), r3(conv_ln_g), r3(conv_ln_b), r3(sgu_ln_g), r3(sgu_ln_b)
    lq1, lk1, lq2, lk2, sg3 = r3(lambda_q1), r3(lambda_k1), r3(lambda_q2), r3(lambda_k2), r3(subln_g)
    bs_full = jnp.repeat(jnp.swapaxes(b_s, 1, 2), HEAD_DIM, axis=2)
    w00 = r3(jnp.repeat(w_s[:, :, 0, 0], HEAD_DIM, axis=1))
    b0 = r3(jnp.repeat(b_s[:, :, 0], HEAD_DIM, axis=1))
    st_t = jnp.swapaxes(state_conv, 1, 2)
    cache_k2 = cache_k.reshape(depth * n_pool * PAGE_SIZE * heads * 2, HEAD_DIM)
    cache_v4 = cache_v.reshape(depth * n_pool, PAGE_SIZE, heads, hw)
    rope = ((_rope_tables(jnp.arange(seq, dtype=jnp.int32)), seq),
            _rope_tables(jnp.full((db,), past, dtype=jnp.int32)))
    q_scale = LOG2E * HEAD_DIM ** -0.5

    hp = x_prompt.reshape(m, d)
    hs = x_sample.reshape(db, d)
    cp_l, ks_l, vs_l, cs_l, chs_l = [], [], [], [], []
    k_all = jnp.zeros((depth, m, att_w), F32)
    v_all = jnp.zeros((depth, m, att_w), F32)
    for l in range(depth):
        lam_init = 0.8 - 0.6 * math.exp(-0.3 * l)
        h = _rmsnorm(hp, norm1_g[l], BF16)
        hsn = _rmsnorm(hs, norm1_g[l], F32)
        inproj = functools.partial(_proj, [h], [hsn], w_in, l, k_sizes=(d,))
        (zab,), (zab_s,) = inproj(n_off=0, n=4 * c, out_p=(BF16,), out_s=(F32,), name="inproj_mix")
        (q_b,), (q_s,) = inproj(n_off=4 * c, n=att_w, out_p=(BF16,), out_s=(F32,), rope=rope, p_scale=q_scale,
                                name="inproj_q")
        (k_all, k_b), (k_s,) = inproj(n_off=4 * c + att_w, n=att_w, out_p=(F32, BF16), out_s=(F32,), rope=rope,
                                      stacked=(depth, k_all), name="inproj_k")
        (v_all, v_b), (v_s,) = inproj(n_off=4 * c + 2 * att_w, n=att_w, out_p=(F32, BF16), out_s=(F32,),
                                      stacked=(depth, v_all), name="inproj_v")
        ya, conv_tail = _conv_mixer(zab.reshape(bsz, seq, 4 * c), w_dw, b_dw3, clg3, clb3, w_pw2_b, l,
                                    bsz=bsz, seq=seq, c=c)
        yb = _chunk_mixer(zab, slg3, slb3, w_s, bs_full, l, c=c)
        yc = _attention(q_b.reshape(bsz, seq, att_w), k_b.reshape(bsz, seq, att_w), v_b.reshape(bsz, seq, att_w),
                        lq1, lk1, lq2, lk2, sg3, l, lam_init, heads=heads)
        ya_pre, st_new, yb_s, chv = _sample_mixers(zab_s, st_t, w_dw, b_dw3, clg3, clb3, slg3, slb3, w00, b0, l, c=c)
        ya_s = _matmul(ya_pre, w_pw2_b, l, tm=db, tn=c, name="s_pw2")
        yc_s = _paged_attention(page_table, q_s.reshape(db, heads, hw), k_s.reshape(db, heads, hw),
                                v_s.reshape(db, heads, hw), cache_k2, cache_v4, lq1, lk1, lq2, lk2, sg3, l, lam_init,
                                heads=heads, n_pool=n_pool)
        (hp,), (hs,) = _proj([ya.reshape(m, c), yb, yc.reshape(m, att_w)], [ya_s, yb_s, yc_s.reshape(db, att_w)],
                             w_o, l, k_sizes=(c, c, att_w), n_off=0, n=d, out_p=(F32,), out_s=(F32,),
                             res=(hp, hs), name="outproj")
        h2 = _rmsnorm(hp, norm2_g[l], BF16)
        hs2 = _rmsnorm(hs, norm2_g[l], F32)
        a_p, a_s = _gateup(h2, hs2, w_gate, w_up, l)
        hp = _matmul(a_p, w_down_b, l, tm=min(512, m), tn=min(512, d), res=hp, name="ffn_down")
        hs = _matmul(a_s, w_down_b, l, tm=db, tn=min(512, d), res=hs, name="s_ffn_down")
        cp_l.append(conv_tail)
        ks_l.append(k_s.reshape(db, 1, heads, 2, HEAD_DIM))
        vs_l.append(v_s.reshape(db, 1, heads, hw))
        cs_l.append(jnp.swapaxes(st_new, 0, 1))
        chs_l.append(chv.reshape(db, 1, c))
    y_prompt = _rmsnorm(hp, final_g, F32).reshape(bsz, seq, d)
    y_sample = _rmsnorm(hs, final_g, F32).reshape(db, 1, d)
    k_prompt = k_all.reshape(depth, bsz, seq, heads, 2, HEAD_DIM)
    v_prompt = v_all.reshape(depth, bsz, seq, heads, hw)
    return (y_prompt, y_sample, k_prompt, v_prompt, jnp.stack(cp_l), jnp.stack(ks_l),
            jnp.stack(vs_l), jnp.stack(cs_l), jnp.stack(chs_l))
```
